```python
import jax
import jax.numpy as jnp
from jax import lax
import numpy as np

D_MODEL = 1024
BATCH = 8
SEQ = 4096
DEPTH = 2

GRID_W = 64
CTX_LEN = 256
N_DIR = 2
N_BRANCH = 2
NORM_EPS = 1e-6

SSM_EXPAND = 2
D_SSM = SSM_EXPAND * D_MODEL
SSM_HEAD_DIM = 64
SSM_HEADS = D_SSM // SSM_HEAD_DIM
SSM_GROUPS = 4
SSM_HEADS_PER_GROUP = SSM_HEADS // SSM_GROUPS
SSM_STATE = 128
SSM_CONV = 3
SSM_CHUNK = 128
D_XBC = D_SSM + 2 * SSM_GROUPS * SSM_STATE

D_RWKV = D_MODEL
RWKV_HEAD = 64
RWKV_HEADS = D_RWKV // RWKV_HEAD
RWKV_CONV = 3
LORA_W = 64
LORA_A = 64
LORA_G = 128
RWKV_LN_EPS = 64e-5

D_FF = 2816
FFN_CONV = 3

IN_SIZES = (D_SSM, D_XBC, SSM_HEADS, 3 * D_RWKV, N_DIR * LORA_W, N_DIR * LORA_A, LORA_G, N_BRANCH * D_MODEL)
D_IN = D_SSM + D_XBC + SSM_HEADS + 3 * D_RWKV + N_DIR * LORA_W + N_DIR * LORA_A + LORA_G + N_BRANCH * D_MODEL

kernel_name = "hybrid_ssd_rwkv7_convffn_dit"


def _split_points(sizes):
    pts, acc = [], 0
    for s in sizes[:-1]:
        acc += s
        pts.append(acc)
    return pts


def group_rmsnorm(x, groups, g):
    shp = x.shape
    xf = x.astype(jnp.float32).reshape(shp[:-1] + (groups, shp[-1] // groups))
    xf = xf * lax.rsqrt(jnp.mean(xf * xf, axis=-1, keepdims=True) + NORM_EPS)
    return xf.reshape(shp).astype(x.dtype) * g


def rmsnorm(x, g):
    return group_rmsnorm(x, 1, g)


def modulate(h, shift, scale):
    return h * (1 + scale) + shift


def adaln(cond, w, b):
    return jnp.split(jax.nn.silu(cond) @ w + b, 6, axis=-1)


def dwconv1d(t, w):
    K, C = w.shape
    return lax.conv_general_dilated(t, w[:, None, :], window_strides=(1,), padding=[(K // 2, K // 2)],
                                    dimension_numbers=('NWC', 'WIO', 'NWC'), feature_group_count=C)


def dwconv2d_grid(t, w):
    b, T, C = t.shape
    g = t.reshape(b, T // GRID_W, GRID_W, C)
    out = lax.conv_general_dilated(g, w[:, :, None, :], window_strides=(1, 1), padding=[(1, 1), (1, 1)],
                                   dimension_numbers=('NHWC', 'HWIO', 'NHWC'), feature_group_count=C)
    return out.reshape(b, T, C)


def seg_conv1d(t, w):
    return jnp.concatenate([dwconv1d(t[:, :CTX_LEN], w), dwconv1d(t[:, CTX_LEN:], w)], axis=1)


def seg_reverse(t):
    return jnp.concatenate([jnp.flip(t[:, :CTX_LEN], axis=1), jnp.flip(t[:, CTX_LEN:], axis=1)], axis=1)


def directional(fn, d, *args):
    if d == 0:
        return fn(*args)
    return seg_reverse(fn(*[seg_reverse(t) for t in args]))


def grid_transpose(t, rows, cols):
    b = t.shape[0]
    rest = t.shape[2:]
    return jnp.swapaxes(t.reshape((b, rows, cols) + rest), 1, 2).reshape((b, rows * cols) + rest)


def ssd_scan(x, dt, A, Bm, Cm):
    b, T, G, J, P = x.shape
    N = Bm.shape[-1]
    Q = SSM_CHUNK
    nc = T // Q
    x = x.reshape(b, nc, Q, G, J, P)
    dt = dt.reshape(b, nc, Q, G, J)
    Bm = Bm.reshape(b, nc, Q, G, N)
    Cm = Cm.reshape(b, nc, Q, G, N)
    xdt = x * dt[..., None]
    cs = jnp.cumsum(jnp.moveaxis(dt * A, 2, -1), axis=-1)
    mask = jnp.tril(jnp.ones((Q, Q), dtype=bool))
    Lmat = jnp.exp(jnp.where(mask, cs[..., :, None] - cs[..., None, :], -jnp.inf))
    cb = jnp.einsum('bclgn,bcsgn->bcgls', Cm, Bm)
    y_diag = jnp.einsum('bcgls,bcgjls,bcsgjp->bclgjp', cb, Lmat, xdt)
    decay_to_end = jnp.exp(cs[..., -1:] - cs)
    states = jnp.einsum('bcsgn,bcgjs,bcsgjp->bcgjpn', Bm, decay_to_end, xdt)
    chunk_decay = jnp.exp(cs[..., -1])

    def step(h, inp):
        st, dec = inp
        return dec[..., None, None] * h + st, h

    h0 = jnp.zeros((b, G, J, P, N), x.dtype)
    _, h_in = lax.scan(step, h0, (jnp.moveaxis(states, 1, 0), jnp.moveaxis(chunk_decay, 1, 0)))
    h_in = jnp.moveaxis(h_in, 0, 1)
    y_off = jnp.einsum('bclgn,bcgjpn,bcgjl->bclgjp', Cm, h_in, jnp.exp(cs))
    return (y_diag + y_off).reshape(b, T, G, J, P)


def rwkv7_scan(r, w, k, v, a, bvec):
    def step(S, inp):
        r_t, w_t, k_t, v_t, a_t, b_t = inp
        sa = jnp.einsum('bhvk,bhk->bhv', S, a_t)
        S = S * w_t[:, :, None, :] + sa[..., None] * b_t[:, :, None, :] + v_t[..., None] * k_t[:, :, None, :]
        return S, jnp.einsum('bhvk,bhk->bhv', S, r_t)

    bsz, T, H, N = r.shape
    S0 = jnp.zeros((bsz, H, N, N), r.dtype)
    seq = [jnp.moveaxis(t, 1, 0) for t in (r, w, k, v, a, bvec)]
    _, y = lax.scan(step, S0, seq)
    return jnp.moveaxis(y, 0, 1)


def token_mixer(h, w_in, ssm_conv_w, ssm_conv_b, ssm_dt_bias, ssm_a_log, ssm_d, ssm_norm, ssm_out,
                rwkv_conv_w, rwkv_w0, rwkv_w2, rwkv_a0, rwkv_a2, rwkv_g2, rwkv_k_k, rwkv_k_a, rwkv_r_k,
                rwkv_ln_w, rwkv_ln_b, rwkv_out, w_o):
    f32 = jnp.float32
    bsz, T, _ = h.shape
    z, xbc, dt_raw, rkv, w_dn, a_dn, g_dn, gate_pre = jnp.split(h @ w_in, _split_points(IN_SIZES), axis=-1)

    xbc = jax.nn.silu(seg_conv1d(xbc, ssm_conv_w) + ssm_conv_b)
    xs, Bm, Cm = jnp.split(xbc, [D_SSM, D_SSM + SSM_GROUPS * SSM_STATE], axis=-1)
    xs_h = xs.astype(f32).reshape(bsz, T, SSM_GROUPS, SSM_HEADS_PER_GROUP, SSM_HEAD_DIM)
    Bm = Bm.astype(f32).reshape(bsz, T, SSM_GROUPS, SSM_STATE)
    Cm = Cm.astype(f32).reshape(bsz, T, SSM_GROUPS, SSM_STATE)
    y_ssm = ssm_d.astype(f32).reshape(SSM_GROUPS, SSM_HEADS_PER_GROUP)[..., None] * xs_h
    for d in range(N_DIR):
        dt = jax.nn.softplus(dt_raw.astype(f32) + ssm_dt_bias[d]).reshape(bsz, T, SSM_GROUPS, SSM_HEADS_PER_GROUP)
        A = -jnp.exp(ssm_a_log[d].astype(f32)).reshape(SSM_GROUPS, SSM_HEADS_PER_GROUP)
        y_ssm = y_ssm + directional(lambda xx, dd, bb, cc: ssd_scan(xx, dd, A, bb, cc), d, xs_h, dt, Bm, Cm)
    y_ssm = y_ssm.reshape(bsz, T, D_SSM).astype(h.dtype) * jax.nn.silu(z)
    out_ssm = group_rmsnorm(y_ssm, SSM_GROUPS, ssm_norm) @ ssm_out

    r, k, v = jnp.split(seg_conv1d(rkv, rwkv_conv_w).astype(f32), 3, axis=-1)

    def heads(t):
        return t.reshape(bsz, T, RWKV_HEADS, RWKV_HEAD)

    kk = heads(k * rwkv_k_k)
    kk = kk / jnp.maximum(jnp.linalg.norm(kk, axis=-1, keepdims=True), 1e-12)
    w_dn = w_dn.reshape(bsz, T, N_DIR, LORA_W)
    a_dn = a_dn.reshape(bsz, T, N_DIR, LORA_A)
    ys, ks = [], []
    for d in range(N_DIR):
        w_log = -jax.nn.softplus(-(rwkv_w0[d] + jnp.tanh(w_dn[:, :, d]) @ rwkv_w2[d])) - 0.5
        decay = jnp.exp(-jnp.exp(w_log.astype(f32)))
        a = jax.nn.sigmoid((rwkv_a0[d] + a_dn[:, :, d] @ rwkv_a2[d]).astype(f32))
        kd = k * (1 + (a - 1) * rwkv_k_a)
        ys.append(directional(rwkv7_scan, d, heads(r), heads(decay), heads(kd), heads(v), -kk, kk * heads(a)))
        ks.append(kd)
    yh = ys[0] + ys[1]
    mu = jnp.mean(yh, axis=-1, keepdims=True)
    var = jnp.mean(jnp.square(yh - mu), axis=-1, keepdims=True)
    yh = (yh - mu) * lax.rsqrt(var + RWKV_LN_EPS)
    y_rw = yh.reshape(bsz, T, D_RWKV) * rwkv_ln_w + rwkv_ln_b
    bonus = jnp.sum(heads(r) * heads(ks[0] + ks[1]) * rwkv_r_k, axis=-1, keepdims=True) * heads(v)
    y_rw = y_rw + bonus.reshape(bsz, T, D_RWKV)
    g = jax.nn.sigmoid(g_dn) @ rwkv_g2
    out_rw = (y_rw.astype(h.dtype) * g) @ rwkv_out

    gate_ssm, gate_rw = jnp.split(jax.nn.sigmoid(gate_pre), N_BRANCH, axis=-1)
    return (gate_ssm * out_ssm + gate_rw * out_rw) @ w_o


def conv_ffn(h, w_up, conv_w, conv_b, w_down, on_grid):
    gate_pre, val = jnp.split(h @ w_up, 2, axis=-1)
    if on_grid:
        gate_pre = dwconv2d_grid(gate_pre, conv_w) + conv_b
    else:
        gate_pre = dwconv1d(gate_pre, conv_w[FFN_CONV // 2]) + conv_b
    return (jax.nn.gelu(gate_pre, approximate=True) * val) @ w_down


def setup_inputs(seed: int = 0) -> dict:
    key = jax.random.key(seed)
    ks = jax.random.split(key, 35)
    f32 = jnp.float32

    def nrm(k, shape, scale):
        return jax.random.normal(k, shape, f32) * scale

    def gain(k, shape):
        return 1.0 + 0.05 * jax.random.normal(k, shape, f32)

    L, D = DEPTH, D_MODEL
    dt0 = jnp.exp(jax.random.uniform(ks[13], (L, N_DIR, SSM_HEADS), f32, np.log(1e-3), np.log(1e-1)))
    return {
        "x": nrm(ks[0], (BATCH, SEQ, D), 1.0),
        "c": nrm(ks[1], (BATCH, D), 1.0),
        "ctx": nrm(ks[2], (BATCH, CTX_LEN, D), 1.0),
        "c_ctx": nrm(ks[3], (D,), 1.0),
        "ada_w": nrm(ks[4], (L, D, 6 * D), 0.5 * D ** -0.5),
        "ada_b": nrm(ks[5], (L, 6 * D), 0.02),
        "norm_mix_pre": gain(ks[6], (L, D)),
        "norm_mix_post": gain(ks[7], (L, D)),
        "norm_ffn_pre": gain(ks[8], (L, D)),
        "norm_ffn_post": gain(ks[9], (L, D)),
        "w_in": nrm(ks[10], (L, D, D_IN), D ** -0.5),
        "ssm_conv_w": nrm(ks[11], (L, SSM_CONV, D_XBC), SSM_CONV ** -0.5),
        "ssm_conv_b": nrm(ks[12], (L, D_XBC), 0.02),
        "ssm_dt_bias": jnp.log(jnp.expm1(dt0)),
        "ssm_a_log": jnp.log(jax.random.uniform(ks[14], (L, N_DIR, SSM_HEADS), f32, 1.0, 16.0)),
        "ssm_d": gain(ks[15], (L, SSM_HEADS)),
        "ssm_norm": gain(ks[16], (L, D_SSM)),
        "ssm_out": nrm(ks[17], (L, D_SSM, D), D_SSM ** -0.5),
        "rwkv_conv_w": nrm(ks[18], (L, RWKV_CONV, 3 * D_RWKV), RWKV_CONV ** -0.5),
        "rwkv_w0": jax.random.uniform(ks[19], (L, N_DIR, D_RWKV), f32, -4.0, 1.0),
        "rwkv_w2": nrm(ks[20], (L, N_DIR, LORA_W, D_RWKV), 0.1 * LORA_W ** -0.5),
        "rwkv_a0": nrm(ks[21], (L, N_DIR, D_RWKV), 0.1),
        "rwkv_a2": nrm(ks[22], (L, N_DIR, LORA_A, D_RWKV), 0.1 * LORA_A ** -0.5),
        "rwkv_g2": nrm(ks[23], (L, LORA_G, D_RWKV), LORA_G ** -0.5),
        "rwkv_k_k": 0.85 + 0.05 * jax.random.normal(ks[24], (L, D_RWKV), f32),
        "rwkv_k_a": gain(ks[25], (L, D_RWKV)),
        "rwkv_r_k": nrm(ks[26], (L, RWKV_HEADS, RWKV_HEAD), 0.1),
        "rwkv_ln_w": gain(ks[27], (L, D_RWKV)),
        "rwkv_ln_b": nrm(ks[28], (L, D_RWKV), 0.02),
        "rwkv_out": nrm(ks[29], (L, D_RWKV, D), D_RWKV ** -0.5),
        "w_o": nrm(ks[30], (L, D, D), D ** -0.5),
        "ffn_w_in": nrm(ks[31], (L, D, 2 * D_FF), D ** -0.5),
        "ffn_conv_w": nrm(ks[32], (L, FFN_CONV, FFN_CONV, D_FF), 1.0 / FFN_CONV),
        "ffn_conv_b": nrm(ks[33], (L, D_FF), 0.02),
        "ffn_w_out": nrm(ks[34], (L, D_FF, D), D_FF ** -0.5),
    }


def reference(x, c, ctx, c_ctx, ada_w, ada_b, norm_mix_pre, norm_mix_post, norm_ffn_pre, norm_ffn_post,
              w_in, ssm_conv_w, ssm_conv_b, ssm_dt_bias, ssm_a_log, ssm_d, ssm_norm, ssm_out,
              rwkv_conv_w, rwkv_w0, rwkv_w2, rwkv_a0, rwkv_a2, rwkv_g2, rwkv_k_k, rwkv_k_a, rwkv_r_k,
              rwkv_ln_w, rwkv_ln_b, rwkv_out, w_o, ffn_w_in, ffn_conv_w, ffn_conv_b, ffn_w_out):
    rows = x.shape[1] // GRID_W
    xl, xc = x, ctx
    for i in range(DEPTH):
        last = i == DEPTH - 1
        mod_l = [m[:, None, :] for m in adaln(c, ada_w[i], ada_b[i])]
        mod_c = adaln(c_ctx, ada_w[i], ada_b[i])

        hl = modulate(rmsnorm(xl, norm_mix_pre[i]), mod_l[0], mod_l[1])
        hc = modulate(rmsnorm(xc, norm_mix_pre[i]), mod_c[0], mod_c[1])
        col_major = i % 2 == 1
        if col_major:
            hl = grid_transpose(hl, rows, GRID_W)
        y = token_mixer(jnp.concatenate([hc, hl], axis=1), w_in[i], ssm_conv_w[i], ssm_conv_b[i],
                        ssm_dt_bias[i], ssm_a_log[i], ssm_d[i], ssm_norm[i], ssm_out[i],
                        rwkv_conv_w[i], rwkv_w0[i], rwkv_w2[i], rwkv_a0[i], rwkv_a2[i], rwkv_g2[i],
                        rwkv_k_k[i], rwkv_k_a[i], rwkv_r_k[i], rwkv_ln_w[i], rwkv_ln_b[i], rwkv_out[i], w_o[i])
        yc, yl = y[:, :CTX_LEN], y[:, CTX_LEN:]
        if col_major:
            yl = grid_transpose(yl, GRID_W, rows)
        xl = xl + mod_l[2] * rmsnorm(yl, norm_mix_post[i])

        hl = modulate(rmsnorm(xl, norm_ffn_pre[i]), mod_l[3], mod_l[4])
        fl = conv_ffn(hl, ffn_w_in[i], ffn_conv_w[i], ffn_conv_b[i], ffn_w_out[i], True)
        xl = xl + mod_l[5] * rmsnorm(fl, norm_ffn_post[i])

        if not last:
            xc = xc + mod_c[2] * rmsnorm(yc, norm_mix_post[i])
            hc = modulate(rmsnorm(xc, norm_ffn_pre[i]), mod_c[3], mod_c[4])
            fc = conv_ffn(hc, ffn_w_in[i], ffn_conv_w[i], ffn_conv_b[i], ffn_w_out[i], False)
            xc = xc + mod_c[5] * rmsnorm(fc, norm_ffn_post[i])
    return xl
```

```python
import functools

import jax
import jax.numpy as jnp
from jax import lax
from jax.experimental import pallas as pl
from jax.experimental.pallas import tpu as pltpu

F32 = jnp.float32
BF16 = jnp.bfloat16

GRID_W = 64
CTX_LEN = 256
NORM_EPS = 1e-6
RWKV_LN_EPS = 64e-5

LANES = 128
SSM_HEAD_DIM = 64
SSM_GROUPS = 4
SSM_STATE = 128
SSM_CHUNK = 128
RWKV_HEAD = 64
RWKV_CHUNK = 64
LORA = 64

ROW_TILE = 256
VMEM_LIMIT = 56 * 1024 * 1024


def _params(sem):
    return pltpu.CompilerParams(dimension_semantics=sem, vmem_limit_bytes=VMEM_LIMIT)


def _bdot(a, b):
    return jnp.dot(a.astype(BF16), b.astype(BF16), preferred_element_type=F32)


def _bdot_nt(a, b):
    return lax.dot_general(a.astype(BF16), b.astype(BF16), (((1,), (1,)), ((), ())),
                           preferred_element_type=F32)


def _bdot_tn(a, b):
    return lax.dot_general(a.astype(BF16), b.astype(BF16), (((0,), (0,)), ((), ())),
                           preferred_element_type=F32)


def _split(x, parts):
    out = []
    r = x
    for _ in range(parts):
        p = r.astype(BF16)
        out.append(p)
        r = r - p.astype(F32)
    return out


def _sel_dot(sel, x, parts):
    acc = None
    for p in _split(x, parts):
        t = jnp.dot(sel, p, preferred_element_type=F32)
        acc = t if acc is None else acc + t
    return acc


def _dot_sel(x, sel, parts):
    acc = None
    for p in _split(x, parts):
        t = jnp.dot(p, sel, preferred_element_type=F32)
        acc = t if acc is None else acc + t
    return acc


def _sigmoid(x):
    return 1.0 / (1.0 + jnp.exp(-x))


def _softplus(x):
    return jnp.maximum(x, 0.0) + jnp.log(1.0 + jnp.exp(-jnp.abs(x)))


def _rms(x):
    return x * lax.rsqrt(jnp.mean(x * x, axis=-1, keepdims=True) + NORM_EPS)


def _half_ones():
    r = lax.broadcasted_iota(jnp.int32, (LANES, LANES), 0) // RWKV_HEAD
    c = lax.broadcasted_iota(jnp.int32, (LANES, LANES), 1) // RWKV_HEAD
    return (r == c).astype(BF16)


def _adaln_kernel(c_ref, w_ref, b_ref, o_ref):
    c = c_ref[...]
    s = c * _sigmoid(c)
    acc = None
    w = w_ref[...]
    w_parts = _split(w, 3)
    s_parts = _split(s, 3)
    for i in range(3):
        for j in range(3 - i):
            t = jnp.dot(s_parts[i], w_parts[j], preferred_element_type=F32)
            acc = t if acc is None else acc + t
    o_ref[...] = acc + b_ref[...]


def _adaln(cond, w, b):
    rows, d = cond.shape
    n = w.shape[1]
    tn = n // 12
    return pl.pallas_call(
        _adaln_kernel,
        grid=(n // tn,),
        in_specs=[pl.BlockSpec((rows, d), lambda j: (0, 0)),
                  pl.BlockSpec((d, tn), lambda j: (0, j)),
                  pl.BlockSpec((1, tn), lambda j: (0, j))],
        out_specs=pl.BlockSpec((rows, tn), lambda j: (0, j)),
        out_shape=jax.ShapeDtypeStruct((rows, n), F32),
        compiler_params=_params(("arbitrary",)),
        name="adaln",
    )(cond, w, b.reshape(1, n))


def _norm_proj_kernel(x_ref, mod_ref, g_ref, w_ref, o_ref, *, shift_row):
    h = _rms(x_ref[...]) * g_ref[...]
    h = h * (1.0 + mod_ref[shift_row + 1:shift_row + 2, :]) + mod_ref[shift_row:shift_row + 1, :]
    o_ref[...] = jnp.dot(h.astype(BF16), w_ref[...], preferred_element_type=F32)


def _norm_proj(x, mods, mod_row, g, w, *, shift_row, tn):
    bsz, rows, d = x.shape
    n = w.shape[1]
    tm = ROW_TILE
    return pl.pallas_call(
        functools.partial(_norm_proj_kernel, shift_row=shift_row),
        grid=(n // tn, bsz, rows // tm),
        in_specs=[pl.BlockSpec((None, tm, d), lambda j, b, t: (b, t, 0)),
                  pl.BlockSpec((None, 6, d), lambda j, b, t: (mod_row(b, t), 0, 0)),
                  pl.BlockSpec((1, d), lambda j, b, t: (0, 0)),
                  pl.BlockSpec((d, tn), lambda j, b, t: (0, j))],
        out_specs=pl.BlockSpec((None, tm, tn), lambda j, b, t: (b, t, j)),
        out_shape=jax.ShapeDtypeStruct((bsz, rows, n), F32),
        compiler_params=_params(("arbitrary", "arbitrary", "arbitrary")),
        name="norm_proj",
    )(x, mods, g.reshape(1, d), w)


def _seg_conv_kernel(x_ref, p_ref, n_ref, w_ref, b_ref, o_ref, *, ctx_tiles, act):
    t = pl.program_id(1)
    nt = pl.num_programs(1)
    x = x_ref[...]
    tm = x.shape[0]
    row = lax.broadcasted_iota(jnp.int32, x.shape, 0)
    first = jnp.logical_or(t == 0, t == ctx_tiles)
    last = jnp.logical_or(t == ctx_tiles - 1, t == nt - 1)
    prev_row = jnp.where(first, 0.0, p_ref[7:8, :])
    next_row = jnp.where(last, 0.0, n_ref[0:1, :])
    xp = jnp.where(row == 0, prev_row, pltpu.roll(x, 1, 0))
    xn = jnp.where(row == tm - 1, next_row, pltpu.roll(x, tm - 1, 0))
    y = w_ref[0:1, :] * xp + w_ref[1:2, :] * x + w_ref[2:3, :] * xn + b_ref[...]
    if act:
        y = y * _sigmoid(y)
    o_ref[...] = y


def _seg_conv(proj, col0, w, b, *, act):
    bsz, tt, _ = proj.shape
    ch = w.shape[1]
    tm, tc = ROW_TILE, 1024
    cb0 = col0 // tc
    hb = tm // 8
    nhb = tt // 8
    return pl.pallas_call(
        functools.partial(_seg_conv_kernel, ctx_tiles=CTX_LEN // tm, act=act),
        grid=(bsz, tt // tm, ch // tc),
        in_specs=[pl.BlockSpec((None, tm, tc), lambda b_, t, c: (b_, t, cb0 + c)),
                  pl.BlockSpec((None, 8, tc), lambda b_, t, c: (b_, jnp.maximum(t * hb - 1, 0), cb0 + c)),
                  pl.BlockSpec((None, 8, tc), lambda b_, t, c: (b_, jnp.minimum((t + 1) * hb, nhb - 1), cb0 + c)),
                  pl.BlockSpec((3, tc), lambda b_, t, c: (0, c)),
                  pl.BlockSpec((1, tc), lambda b_, t, c: (0, c))],
        out_specs=pl.BlockSpec((None, tm, tc), lambda b_, t, c: (b_, t, c)),
        out_shape=jax.ShapeDtypeStruct((bsz, tt, ch), F32),
        compiler_params=_params(("arbitrary", "arbitrary", "arbitrary")),
        name="seg_conv",
    )(proj, proj, proj, w, b.reshape(1, ch))


def _pair_expand(v, h0):
    rows = v.shape[0]
    lane = lax.broadcasted_iota(jnp.int32, (rows, LANES), 1)
    lo = jnp.broadcast_to(v[:, h0:h0 + 1], (rows, LANES))
    hi = jnp.broadcast_to(v[:, h0 + 1:h0 + 2], (rows, LANES))
    return jnp.where(lane < SSM_HEAD_DIM, lo, hi)


def _ssd_kernel(xs_ref, b_ref, c_ref, dt_ref, bias_ref, alog_ref, o_ref, h_ref):
    d = pl.program_id(1)
    s = pl.program_id(2)
    q = SSM_CHUNK
    fwd = d == 0

    @pl.when(s == 0)
    def _():
        h_ref[...] = jnp.zeros_like(h_ref)

    li = lax.broadcasted_iota(jnp.int32, (q, q), 0)
    si = lax.broadcasted_iota(jnp.int32, (q, q), 1)
    sgn = jnp.where(fwd, 1, -1)
    incl = sgn * (si - li) <= 0
    tri = jnp.where(incl, 1.0, 0.0).astype(BF16)
    lane = lax.broadcasted_iota(jnp.int32, (q, LANES), 1)
    half = lane < SSM_HEAD_DIM

    dt = _softplus(dt_ref[...] + bias_ref[...])
    da = dt * (-jnp.exp(alog_ref[...]))
    cs = _sel_dot(tri, da, 3)
    cs_t = cs.T
    cs_end = jnp.where(fwd, cs[q - 1:q, :], cs[0:1, :])
    w_end = dt * jnp.exp(cs_end - cs)
    e_cs = jnp.exp(cs)
    c_dec = jnp.exp(cs_end)

    n_pairs = SSM_GROUPS * 4
    for g in range(SSM_GROUPS):
        bg = b_ref[:, g * SSM_STATE:(g + 1) * SSM_STATE]
        cg = c_ref[:, g * SSM_STATE:(g + 1) * SSM_STATE]
        cb = _bdot_nt(cg, bg)
        bg_t = bg.T
        for pp in range(n_pairs // SSM_GROUPS):
            p = g * 4 + pp
            h0 = 2 * p
            sl = slice(p * LANES, (p + 1) * LANES)
            xs = xs_ref[:, sl]
            xdt = xs * _pair_expand(dt, h0)
            h_old = h_ref[:, sl]
            y = _bdot(cg, h_old) * _pair_expand(e_cs, h0)
            for k in range(2):
                hd = h0 + k
                dl = jnp.broadcast_to(cs[:, hd:hd + 1], (q, q))
                ds_ = jnp.broadcast_to(cs_t[hd:hd + 1, :], (q, q))
                lm = jnp.exp(jnp.where(incl, dl - ds_, -1e30))
                msk = half if k == 0 else jnp.logical_not(half)
                y = y + _bdot(cb * lm, jnp.where(msk, xdt, 0.0))
            o_ref[:, sl] = y
            h_ref[:, sl] = h_old * _pair_expand(c_dec, h0) + _bdot(bg_t, xs * _pair_expand(w_end, h0))


def _ssd_scan(xbc, proj, dt_col, dt_bias, a_log):
    bsz, tt, _ = xbc.shape
    q = SSM_CHUNK
    nc = tt // q
    ncx = CTX_LEN // q
    d_ssm = xbc.shape[2] - 2 * SSM_GROUPS * SSM_STATE
    bc_w = SSM_GROUPS * SSM_STATE

    def chunk(d, s):
        rev = jnp.where(s < ncx, ncx - 1 - s, nc - 1 + ncx - s)
        return jnp.where(d == 0, s, rev)

    return pl.pallas_call(
        _ssd_kernel,
        grid=(bsz, 2, nc),
        in_specs=[pl.BlockSpec((None, q, d_ssm), lambda b, d, s: (b, chunk(d, s), 0)),
                  pl.BlockSpec((None, q, bc_w), lambda b, d, s: (b, chunk(d, s), d_ssm // bc_w)),
                  pl.BlockSpec((None, q, bc_w), lambda b, d, s: (b, chunk(d, s), d_ssm // bc_w + 1)),
                  pl.BlockSpec((None, q, LANES), lambda b, d, s: (b, chunk(d, s), dt_col // LANES)),
                  pl.BlockSpec((None, 1, LANES), lambda b, d, s: (d, 0, 0)),
                  pl.BlockSpec((None, 1, LANES), lambda b, d, s: (d, 0, 0))],
        out_specs=pl.BlockSpec((None, None, q, d_ssm), lambda b, d, s: (d, b, chunk(d, s), 0)),
        out_shape=jax.ShapeDtypeStruct((2, bsz, tt, d_ssm), F32),
        scratch_shapes=[pltpu.VMEM((SSM_STATE, d_ssm), F32)],
        compiler_params=_params(("arbitrary", "arbitrary", "arbitrary")),
        name="ssd_scan",
    )(xbc, xbc, xbc, proj, dt_bias, a_log)


def _stack2(x, half):
    return jnp.concatenate([jnp.where(half, x, 0.0), jnp.where(half, 0.0, x)], axis=0)


def _rwkv_kernel(r_ref, k_ref, v_ref, wdn_ref, adn_ref, w2_ref, a2_ref, w0_ref, a0_ref,
                 kk_ref, ka_ref, rk_ref, y_ref, bonus_ref, st_ref):
    d = pl.program_id(1)
    s = pl.program_id(2)
    fwd = d == 0
    c_len = RWKV_CHUNK
    n2 = 2 * c_len

    @pl.when(s == 0)
    def _():
        st_ref[...] = jnp.zeros_like(st_ref)

    ti = lax.broadcasted_iota(jnp.int32, (c_len, c_len), 0)
    ui = lax.broadcasted_iota(jnp.int32, (c_len, c_len), 1)
    sgn = jnp.where(fwd, 1, -1)
    tri = jnp.where(sgn * (ui - ti) <= 0, 1.0, 0.0).astype(BF16)
    ones2 = _half_ones()
    half = lax.broadcasted_iota(jnp.int32, (c_len, LANES), 1) < RWKV_HEAD

    rr = lax.broadcasted_iota(jnp.int32, (n2, n2), 0)
    cc = lax.broadcasted_iota(jnp.int32, (n2, n2), 1)
    same = (rr // c_len) == (cc // c_len)
    order = sgn * (cc % c_len - rr % c_len)
    strict = jnp.logical_and(same, order < 0)
    incl = jnp.logical_and(same, order <= 0)
    eye = rr == cc

    tanh_w = jnp.tanh(wdn_ref[...])
    adn = adn_ref[...]

    n_pairs = r_ref.shape[1] // LANES
    for p in range(n_pairs):
        sl = slice(p * LANES, (p + 1) * LANES)
        r = r_ref[:, sl]
        k = k_ref[:, sl]
        v = v_ref[:, sl]
        z = w0_ref[:, sl] + _bdot(tanh_w, w2_ref[:, sl])
        lw = -jnp.exp(-_softplus(-z) - 0.5)
        a_sig = _sigmoid(a0_ref[:, sl] + _bdot(adn, a2_ref[:, sl]))
        kd = k * (1.0 + (a_sig - 1.0) * ka_ref[:, sl])
        kkv = k * kk_ref[:, sl]
        nrm = jnp.sqrt(_dot_sel(kkv * kkv, ones2, 2))
        kk = kkv / jnp.maximum(nrm, 1e-12)
        bvec = kk * a_sig
        bonus_ref[:, sl] = _dot_sel(r * kd * rk_ref[:, sl], ones2, 2) * v

        c = _sel_dot(tri, lw, 3)
        e_c = jnp.exp(c)
        e_nc = jnp.exp(-c)
        c_end = jnp.where(fwd, c[c_len - 1:c_len, :], c[0:1, :])
        p_end = jnp.exp(c_end)
        r_t = r * e_c
        k_t = kd * e_nc
        b_t = bvec * e_nc
        a_t = -kk * jnp.exp(c - lw)

        at_s = _stack2(a_t, half)
        rt_s = _stack2(r_t, half)
        bt_s = _stack2(b_t, half)
        kt_s = _stack2(k_t, half)
        v_s = _stack2(v, half)
        kc_s = _stack2(k_t * p_end, half)
        bc_s = _stack2(b_t * p_end, half)

        a_ab = jnp.where(strict, _bdot_nt(at_s, bt_s), 0.0)
        a_ak = jnp.where(strict, _bdot_nt(at_s, kt_s), 0.0)
        a_rb = jnp.where(incl, _bdot_nt(rt_s, bt_s), 0.0)
        a_rk = jnp.where(incl, _bdot_nt(rt_s, kt_s), 0.0)

        pw = a_ab
        inv = jnp.where(eye, 1.0, 0.0) + a_ab
        for _ in range(5):
            pw = _bdot(pw, pw)
            inv = inv + _bdot(inv, pw)

        w_m = _bdot(inv, at_s)
        u0 = _bdot(inv, _bdot(a_ak, v_s))
        r_w = rt_s + _bdot(a_rb, w_m)
        y0 = _bdot(a_rk, v_s) + _bdot(a_rb, u0)
        g_m = jnp.where(eye, jnp.broadcast_to(p_end, (n2, n2)), 0.0) + _bdot_tn(bc_s, w_m)
        h_m = _bdot_tn(kc_s, v_s) + _bdot_tn(bc_s, u0)

        st = st_ref[p]
        ys = _bdot(r_w, st) + y0
        st_ref[p] = _bdot(g_m, st) + h_m
        y_ref[:, sl] = ys[0:c_len, :] + ys[c_len:n2, :]


def _rwkv_scan(rkv, proj, wdn_col, adn_col, w2p, a2p, w0, a0, k_k, k_a, r_k):
    bsz, tt, d3 = rkv.shape
    dr = d3 // 3
    cl = RWKV_CHUNK
    nc = tt // cl
    ncx = CTX_LEN // cl

    def chunk(d, s):
        rev = jnp.where(s < ncx, ncx - 1 - s, nc - 1 + ncx - s)
        return jnp.where(d == 0, s, rev)

    row = lambda: pl.BlockSpec((1, dr), lambda b, d, s: (0, 0))
    out_spec = pl.BlockSpec((None, None, cl, dr), lambda b, d, s: (d, b, chunk(d, s), 0))
    return pl.pallas_call(
        _rwkv_kernel,
        grid=(bsz, 2, nc),
        in_specs=[pl.BlockSpec((None, cl, dr), lambda b, d, s: (b, chunk(d, s), 0)),
                  pl.BlockSpec((None, cl, dr), lambda b, d, s: (b, chunk(d, s), 1)),
                  pl.BlockSpec((None, cl, dr), lambda b, d, s: (b, chunk(d, s), 2)),
                  pl.BlockSpec((None, cl, LANES), lambda b, d, s: (b, chunk(d, s), wdn_col // LANES)),
                  pl.BlockSpec((None, cl, LANES), lambda b, d, s: (b, chunk(d, s), adn_col // LANES)),
                  pl.BlockSpec((None, LANES, dr), lambda b, d, s: (d, 0, 0)),
                  pl.BlockSpec((None, LANES, dr), lambda b, d, s: (d, 0, 0)),
                  pl.BlockSpec((None, 1, dr), lambda b, d, s: (d, 0, 0)),
                  pl.BlockSpec((None, 1, dr), lambda b, d, s: (d, 0, 0)),
                  row(), row(), row()],
        out_specs=[out_spec, out_spec],
        out_shape=[jax.ShapeDtypeStruct((2, bsz, tt, dr), F32)] * 2,
        scratch_shapes=[pltpu.VMEM((dr // LANES, LANES, LANES), F32)],
        compiler_params=_params(("arbitrary", "arbitrary", "arbitrary")),
        name="rwkv_scan",
    )(rkv, rkv, rkv, proj, proj, w2p, a2p, w0, a0,
      k_k.reshape(1, dr), k_a.reshape(1, dr), r_k.reshape(1, dr))


def _merge_kernel(z_ref, xs_ref, yf_ref, yb_ref, r0_ref, r1_ref, b0_ref, b1_ref, gdn_ref, gate_ref,
                  mod_ref, dexp_ref, snorm_ref, sout_ref, lnw_ref, lnb_ref, g2_ref, rout_ref, wo_ref,
                  gpost_ref, o_ref):
    d_model = o_ref.shape[1]
    z = z_ref[...]
    y = dexp_ref[...] * xs_ref[...] + yf_ref[...] + yb_ref[...]
    y = y * (z * _sigmoid(z))
    gw = y.shape[1] // SSM_GROUPS
    yn = jnp.concatenate([_rms(y[:, g * gw:(g + 1) * gw]) for g in range(SSM_GROUPS)], axis=1)
    out_ssm = _bdot(yn * snorm_ref[...], sout_ref[...])

    yh = r0_ref[...] + r1_ref[...]
    ones2 = _half_ones()
    pieces = []
    for p in range(d_model // LANES):
        t = yh[:, p * LANES:(p + 1) * LANES]
        mu = _dot_sel(t, ones2, 2) * (1.0 / RWKV_HEAD)
        tc = t - mu
        var = _dot_sel(tc * tc, ones2, 2) * (1.0 / RWKV_HEAD)
        pieces.append(tc * lax.rsqrt(var + RWKV_LN_EPS))
    y_rw = jnp.concatenate(pieces, axis=1) * lnw_ref[...] + lnb_ref[...] + b0_ref[...] + b1_ref[...]
    g = _bdot(_sigmoid(gdn_ref[...]), g2_ref[...])
    out_rw = _bdot(y_rw * g, rout_ref[...])

    gate = _sigmoid(gate_ref[...])
    merged = gate[:, :d_model] * out_ssm + gate[:, d_model:] * out_rw
    yo = _bdot(merged, wo_ref[...])
    o_ref[...] = mod_ref[2:3, :] * (_rms(yo) * gpost_ref[...])


def _merge(proj, xbc, y_ssd, y_rw, bonus, mods, mod_row, cols, d_exp, ssm_norm, ssm_out,
           ln_w, ln_b, g2, rwkv_out, w_o, g_post):
    bsz, tt, _ = proj.shape
    d_ssm = y_ssd.shape[-1]
    d = y_rw.shape[-1]
    tm = ROW_TILE
    full = lambda shape: pl.BlockSpec(shape, lambda b, t: (0,) * len(shape))
    dir_spec = lambda di, w: pl.BlockSpec((None, None, tm, w), lambda b, t: (di, b, t, 0))
    return pl.pallas_call(
        _merge_kernel,
        grid=(bsz, tt // tm),
        in_specs=[pl.BlockSpec((None, tm, d_ssm), lambda b, t: (b, t, cols["z"] // d_ssm)),
                  pl.BlockSpec((None, tm, d_ssm), lambda b, t: (b, t, 0)),
                  dir_spec(0, d_ssm), dir_spec(1, d_ssm),
                  dir_spec(0, d), dir_spec(1, d), dir_spec(0, d), dir_spec(1, d),
                  pl.BlockSpec((None, tm, LANES), lambda b, t: (b, t, cols["g_dn"] // LANES)),
                  pl.BlockSpec((None, tm, 2 * d), lambda b, t: (b, t, cols["gate"] // (2 * d))),
                  pl.BlockSpec((None, 6, d), lambda b, t: (mod_row(b, t), 0, 0)),
                  full((1, d_ssm)), full((1, d_ssm)), full((d_ssm, d)),
                  full((1, d)), full((1, d)), full((LANES, d)), full((d, d)), full((d, d)), full((1, d))],
        out_specs=pl.BlockSpec((None, tm, d), lambda b, t: (b, t, 0)),
        out_shape=jax.ShapeDtypeStruct((bsz, tt, d), F32),
        compiler_params=_params(("arbitrary", "arbitrary")),
        name="merge",
    )(proj, xbc, y_ssd, y_ssd, y_rw, y_rw, bonus, bonus, proj, proj, mods,
      d_exp, ssm_norm.reshape(1, d_ssm), ssm_out, ln_w.reshape(1, d), ln_b.reshape(1, d), g2,
      rwkv_out, w_o, g_post.reshape(1, d))


def _ffn_up_kernel(x_ref, mix_ref, mod_ref, g_ref, w_ref, xo_ref, up_ref):
    xn = x_ref[...] + mix_ref[...]
    xo_ref[...] = xn
    h = _rms(xn) * g_ref[...]
    h = h * (1.0 + mod_ref[4:5, :]) + mod_ref[3:4, :]
    up_ref[...] = jnp.dot(h.astype(BF16), w_ref[...], preferred_element_type=F32)


def _ffn_up(x, mix, mix_row0, mods, mod_row, g, w):
    bsz, rows, d = x.shape
    n = w.shape[1]
    tm = ROW_TILE
    t0 = mix_row0 // tm
    return pl.pallas_call(
        _ffn_up_kernel,
        grid=(bsz, rows // tm),
        in_specs=[pl.BlockSpec((None, tm, d), lambda b, t: (b, t, 0)),
                  pl.BlockSpec((None, tm, d), lambda b, t: (b, t0 + t, 0)),
                  pl.BlockSpec((None, 6, d), lambda b, t: (mod_row(b, t), 0, 0)),
                  pl.BlockSpec((1, d), lambda b, t: (0, 0)),
                  pl.BlockSpec((d, n), lambda b, t: (0, 0))],
        out_specs=[pl.BlockSpec((None, tm, d), lambda b, t: (b, t, 0)),
                   pl.BlockSpec((None, tm, n), lambda b, t: (b, t, 0))],
        out_shape=[jax.ShapeDtypeStruct((bsz, rows, d), F32), jax.ShapeDtypeStruct((bsz, rows, n), F32)],
        compiler_params=_params(("arbitrary", "arbitrary")),
        name="ffn_up",
    )(x, mix, mods, g.reshape(1, d), w)


def _gelu_tanh(x):
    return 0.5 * x * (1.0 + jnp.tanh(0.7978845608028654 * (x + 0.044715 * x * x * x)))


def _ffn_down_kernel(gate_ref, val_ref, top_ref, bot_ref, cw_ref, cb_ref, wd_ref, x_ref, mod_ref, g_ref,
                     o_ref, act_ref, *, width, chunk):
    t = pl.program_id(1)
    nt = pl.num_programs(1)
    tm, ch = gate_ref.shape
    n_ext = tm + 2 * width
    col = lax.broadcasted_iota(jnp.int32, (n_ext, chunk), 0) % width
    top_on = jnp.where(t == 0, 0.0, 1.0)
    bot_on = jnp.where(t == nt - 1, 0.0, 1.0)
    for c in range(ch // chunk):
        sl = slice(c * chunk, (c + 1) * chunk)
        ext = jnp.concatenate([top_ref[:, sl] * top_on, gate_ref[:, sl], bot_ref[:, sl] * bot_on], axis=0)
        left = jnp.where(col == 0, 0.0, pltpu.roll(ext, 1, 0))
        right = jnp.where(col == width - 1, 0.0, pltpu.roll(ext, n_ext - 1, 0))
        acc = jnp.broadcast_to(cb_ref[:, sl], (tm, chunk))
        for dy in range(3):
            rs = slice(dy * width, dy * width + tm)
            acc = acc + cw_ref[3 * dy:3 * dy + 1, sl] * left[rs]
            acc = acc + cw_ref[3 * dy + 1:3 * dy + 2, sl] * ext[rs]
            acc = acc + cw_ref[3 * dy + 2:3 * dy + 3, sl] * right[rs]
        act_ref[:, sl] = (_gelu_tanh(acc) * val_ref[:, sl]).astype(BF16)
    f = jnp.dot(act_ref[...], wd_ref[...], preferred_element_type=F32)
    o_ref[...] = x_ref[...] + mod_ref[5:6, :] * (_rms(f) * g_ref[...])


def _ffn_down(up, x, mods, mod_row, conv_w, conv_b, w_down, g, *, width):
    bsz, rows, n2 = up.shape
    f = n2 // 2
    d = x.shape[2]
    tm = ROW_TILE
    rpt = tm // width
    nrast = rows // width
    full = lambda shape: pl.BlockSpec(shape, lambda b, t: (0,) * len(shape))
    return pl.pallas_call(
        functools.partial(_ffn_down_kernel, width=width, chunk=256),
        grid=(bsz, rows // tm),
        in_specs=[pl.BlockSpec((None, tm, f), lambda b, t: (b, t, 0)),
                  pl.BlockSpec((None, tm, f), lambda b, t: (b, t, 1)),
                  pl.BlockSpec((None, width, f), lambda b, t: (b, jnp.maximum(t * rpt - 1, 0), 0)),
                  pl.BlockSpec((None, width, f), lambda b, t: (b, jnp.minimum((t + 1) * rpt, nrast - 1), 0)),
                  full((9, f)), full((1, f)), full((f, d)),
                  pl.BlockSpec((None, tm, d), lambda b, t: (b, t, 0)),
                  pl.BlockSpec((None, 6, d), lambda b, t: (mod_row(b, t), 0, 0)),
                  full((1, d))],
        out_specs=pl.BlockSpec((None, tm, d), lambda b, t: (b, t, 0)),
        out_shape=jax.ShapeDtypeStruct((bsz, rows, d), F32),
        scratch_shapes=[pltpu.VMEM((tm, f), BF16)],
        compiler_params=_params(("arbitrary", "arbitrary")),
        name="ffn_down",
    )(up, up, up, up, conv_w.reshape(9, f), conv_b.reshape(1, f), w_down, x, mods, g.reshape(1, d))


def _grid_transpose(t, rows, cols):
    b = t.shape[0]
    rest = t.shape[2:]
    return jnp.swapaxes(t.reshape((b, rows, cols) + rest), 1, 2).reshape((b, rows * cols) + rest)


def kernel(x, c, ctx, c_ctx, ada_w, ada_b, norm_mix_pre, norm_mix_post, norm_ffn_pre, norm_ffn_post, w_in, ssm_conv_w, ssm_conv_b, ssm_dt_bias, ssm_a_log, ssm_d, ssm_norm, ssm_out, rwkv_conv_w, rwkv_w0, rwkv_w2, rwkv_a0, rwkv_a2, rwkv_g2, rwkv_k_k, rwkv_k_a, rwkv_r_k, rwkv_ln_w, rwkv_ln_b, rwkv_out, w_o, ffn_w_in, ffn_conv_w, ffn_conv_b, ffn_w_out):
    bsz, seq, d = x.shape
    depth = ada_w.shape[0]
    rows = seq // GRID_W
    ssm_heads = ssm_d.shape[1]
    d_ssm = ssm_heads * SSM_HEAD_DIM
    d_xbc = d_ssm + 2 * SSM_GROUPS * SSM_STATE
    lora_g = rwkv_g2.shape[1]
    ctx_tiles = CTX_LEN // ROW_TILE
    ctx_row = bsz

    cols = {"z": 0, "xbc": d_ssm, "rkv": d_ssm + d_xbc}
    cols["gate"] = cols["rkv"] + 3 * d
    cols["w_dn"] = cols["gate"] + 2 * d
    cols["a_dn"] = cols["w_dn"] + 2 * LORA
    cols["g_dn"] = cols["a_dn"] + 2 * LORA
    cols["dt"] = cols["g_dn"] + lora_g
    o_z, o_xbc, o_dt, o_rkv = 0, d_ssm, d_ssm + d_xbc, d_ssm + d_xbc + ssm_heads
    o_wdn = o_rkv + 3 * d
    o_adn = o_wdn + 2 * LORA
    o_gdn = o_adn + 2 * LORA
    o_gate = o_gdn + lora_g

    cond = jnp.concatenate([c, c_ctx[None, :], jnp.zeros((16 - bsz - 1, d), F32)], axis=0)
    seq_row = lambda b, t: jnp.where(t < ctx_tiles, ctx_row, b)
    lat_row = lambda b, t: b
    ctx_only_row = lambda b, t: ctx_row

    xl, xc = x, ctx
    for i in range(depth):
        last = i == depth - 1
        col_major = i % 2 == 1
        mods = _adaln(cond, ada_w[i], ada_b[i]).reshape(16, 6, d)

        wi = w_in[i]
        w_cat = jnp.concatenate(
            [wi[:, o_z:o_xbc], wi[:, o_xbc:o_dt], wi[:, o_rkv:o_wdn], wi[:, o_gate:o_gate + 2 * d],
             wi[:, o_wdn:o_adn], wi[:, o_adn:o_gdn], wi[:, o_gdn:o_gate], wi[:, o_dt:o_rkv],
             jnp.zeros((d, LANES - ssm_heads), F32)], axis=1).astype(BF16)

        xl_in = _grid_transpose(xl, rows, GRID_W) if col_major else xl
        x_cat = jnp.concatenate([xc, xl_in], axis=1)
        proj = _norm_proj(x_cat, mods, seq_row, norm_mix_pre[i], w_cat, shift_row=0, tn=w_cat.shape[1] // 7)

        xbc = _seg_conv(proj, cols["xbc"], ssm_conv_w[i], ssm_conv_b[i], act=True)
        rkv = _seg_conv(proj, cols["rkv"], rwkv_conv_w[i], jnp.zeros((3 * d,), F32), act=False)

        pad_heads = lambda a: jnp.pad(a, ((0, 0), (0, LANES - ssm_heads))).reshape(2, 1, LANES)
        y_ssd = _ssd_scan(xbc, proj, cols["dt"], pad_heads(ssm_dt_bias[i]), pad_heads(ssm_a_log[i]))

        zpad = jnp.zeros((LORA, d), F32)
        w2p = jnp.stack([jnp.concatenate([rwkv_w2[i, 0], zpad], 0), jnp.concatenate([zpad, rwkv_w2[i, 1]], 0)])
        a2p = jnp.stack([jnp.concatenate([rwkv_a2[i, 0], zpad], 0), jnp.concatenate([zpad, rwkv_a2[i, 1]], 0)])
        y_rw, bonus = _rwkv_scan(rkv, proj, cols["w_dn"], cols["a_dn"], w2p.astype(BF16), a2p.astype(BF16),
                                 rwkv_w0[i].reshape(2, 1, d), rwkv_a0[i].reshape(2, 1, d),
                                 rwkv_k_k[i], rwkv_k_a[i], rwkv_r_k[i])

        d_exp = jnp.repeat(ssm_d[i], SSM_HEAD_DIM).reshape(1, d_ssm)
        mix = _merge(proj, xbc, y_ssd, y_rw, bonus, mods, seq_row, cols, d_exp, ssm_norm[i],
                     ssm_out[i].astype(BF16), rwkv_ln_w[i], rwkv_ln_b[i], rwkv_g2[i].astype(BF16),
                     rwkv_out[i].astype(BF16), w_o[i].astype(BF16), norm_mix_post[i])

        w_up = ffn_w_in[i].astype(BF16)
        w_dn = ffn_w_out[i].astype(BF16)
        if col_major:
            xl, up = _ffn_up(xl, _grid_transpose(mix[:, CTX_LEN:], GRID_W, rows), 0, mods, lat_row,
                             norm_ffn_pre[i], w_up)
        else:
            xl, up = _ffn_up(xl, mix, CTX_LEN, mods, lat_row, norm_ffn_pre[i], w_up)
        xl = _ffn_down(up, xl, mods, lat_row, ffn_conv_w[i], ffn_conv_b[i], w_dn, norm_ffn_post[i], width=GRID_W)

        if not last:
            xc, up_c = _ffn_up(xc, mix, 0, mods, ctx_only_row, norm_ffn_pre[i], w_up)
            xc = _ffn_down(up_c, xc, mods, ctx_only_row, ffn_conv_w[i], ffn_conv_b[i], w_dn, norm_ffn_post[i],
                           width=CTX_LEN)
    return xl
```

```python
import functools

import jax
import jax.numpy as jnp
from jax import lax
from jax.experimental import pallas as pl
from jax.experimental.pallas import tpu as pltpu

F32 = jnp.float32
BF16 = jnp.bfloat16

GRID_W = 64
CTX_LEN = 256
NORM_EPS = 1e-6
RWKV_LN_EPS = 64e-5

LANES = 128
SSM_HEAD_DIM = 64
SSM_GROUPS = 4
SSM_STATE = 128
SSM_CHUNK = 128
RWKV_HEAD = 64
RWKV_CHUNK = 64
RWKV_BATCH = 2
RWKV_WAVE = 2
LORA = 64

ROW_TILE = 256
VMEM_LIMIT = 56 * 1024 * 1024


def _params(sem):
    return pltpu.CompilerParams(dimension_semantics=sem, vmem_limit_bytes=VMEM_LIMIT)


def _bdot(a, b):
    return jnp.dot(a.astype(BF16), b.astype(BF16), preferred_element_type=F32)


def _bdot_nt(a, b):
    return lax.dot_general(a.astype(BF16), b.astype(BF16), (((1,), (1,)), ((), ())),
                           preferred_element_type=F32)


def _bdot_tn(a, b):
    return lax.dot_general(a.astype(BF16), b.astype(BF16), (((0,), (0,)), ((), ())),
                           preferred_element_type=F32)


def _split(x, parts):
    out = []
    r = x
    for _ in range(parts):
        p = r.astype(BF16)
        out.append(p)
        r = r - p.astype(F32)
    return out


def _sel_dot(sel, x, parts):
    acc = None
    for p in _split(x, parts):
        t = jnp.dot(sel, p, preferred_element_type=F32)
        acc = t if acc is None else acc + t
    return acc


def _dot_sel(x, sel, parts):
    acc = None
    for p in _split(x, parts):
        t = jnp.dot(p, sel, preferred_element_type=F32)
        acc = t if acc is None else acc + t
    return acc


def _sigmoid(x):
    return 1.0 / (1.0 + jnp.exp(-x))


def _softplus(x):
    return jnp.maximum(x, 0.0) + jnp.log(1.0 + jnp.exp(-jnp.abs(x)))


def _rms(x):
    return x * lax.rsqrt(jnp.mean(x * x, axis=-1, keepdims=True) + NORM_EPS)


def _half_ones():
    r = lax.broadcasted_iota(jnp.int32, (LANES, LANES), 0) // RWKV_HEAD
    c = lax.broadcasted_iota(jnp.int32, (LANES, LANES), 1) // RWKV_HEAD
    return (r == c).astype(BF16)


def _adaln_kernel(c_ref, w_ref, b_ref, o_ref):
    c = c_ref[...]
    s = c * _sigmoid(c)
    acc = None
    w = w_ref[...]
    w_parts = _split(w, 3)
    s_parts = _split(s, 3)
    for i in range(3):
        for j in range(3 - i):
            t = jnp.dot(s_parts[i], w_parts[j], preferred_element_type=F32)
            acc = t if acc is None else acc + t
    o_ref[...] = acc + b_ref[...]


def _adaln(cond, w, b):
    rows, d = cond.shape
    n = w.shape[1]
    tn = n // 12
    return pl.pallas_call(
        _adaln_kernel,
        grid=(n // tn,),
        in_specs=[pl.BlockSpec((rows, d), lambda j: (0, 0)),
                  pl.BlockSpec((d, tn), lambda j: (0, j)),
                  pl.BlockSpec((1, tn), lambda j: (0, j))],
        out_specs=pl.BlockSpec((rows, tn), lambda j: (0, j)),
        out_shape=jax.ShapeDtypeStruct((rows, n), F32),
        compiler_params=_params(("arbitrary",)),
        name="adaln",
    )(cond, w, b.reshape(1, n))


def _norm_proj_kernel(x_ref, modb_ref, modc_ref, g_ref, w_ref, o_ref, *, ctx_rows):
    x = x_ref[...]
    tm = x.shape[0]
    h = _rms(x) * g_ref[...]
    is_ctx = pl.program_id(2) * tm + lax.broadcasted_iota(jnp.int32, (tm, 1), 0) < ctx_rows
    shift = jnp.where(is_ctx, modc_ref[0:1, :], modb_ref[0:1, :])
    scale = jnp.where(is_ctx, modc_ref[1:2, :], modb_ref[1:2, :])
    h = h * (1.0 + scale) + shift
    o_ref[...] = jnp.dot(h.astype(BF16), w_ref[...], preferred_element_type=F32)


def _row_tile(rows, cap):
    return max(t for t in range(8, cap + 1, 8) if rows % t == 0)


def _norm_proj(x, mods, ctx_row, g, w, *, tn):
    bsz, rows, d = x.shape
    n = w.shape[1]
    tm = _row_tile(rows, 1088)
    return pl.pallas_call(
        functools.partial(_norm_proj_kernel, ctx_rows=CTX_LEN),
        grid=(n // tn, bsz, rows // tm),
        in_specs=[pl.BlockSpec((None, tm, d), lambda j, b, t: (b, t, 0)),
                  pl.BlockSpec((None, 6, d), lambda j, b, t: (b, 0, 0)),
                  pl.BlockSpec((None, 6, d), lambda j, b, t: (ctx_row, 0, 0)),
                  pl.BlockSpec((1, d), lambda j, b, t: (0, 0)),
                  pl.BlockSpec((d, tn), lambda j, b, t: (0, j))],
        out_specs=pl.BlockSpec((None, tm, tn), lambda j, b, t: (b, t, j)),
        out_shape=jax.ShapeDtypeStruct((bsz, rows, n), F32),
        compiler_params=_params(("arbitrary", "arbitrary", "arbitrary")),
        name="norm_proj",
    )(x, mods, mods, g.reshape(1, d), w)


def _seg_conv_kernel(x_ref, p_ref, n_ref, w_ref, b_ref, o_ref, *, ctx_tiles, act):
    t = pl.program_id(1)
    nt = pl.num_programs(1)
    x = x_ref[...]
    tm = x.shape[0]
    row = lax.broadcasted_iota(jnp.int32, x.shape, 0)
    first = jnp.logical_or(t == 0, t == ctx_tiles)
    last = jnp.logical_or(t == ctx_tiles - 1, t == nt - 1)
    prev_row = jnp.where(first, 0.0, p_ref[7:8, :])
    next_row = jnp.where(last, 0.0, n_ref[0:1, :])
    xp = jnp.where(row == 0, prev_row, pltpu.roll(x, 1, 0))
    xn = jnp.where(row == tm - 1, next_row, pltpu.roll(x, tm - 1, 0))
    y = w_ref[0:1, :] * xp + w_ref[1:2, :] * x + w_ref[2:3, :] * xn + b_ref[...]
    if act:
        y = y * _sigmoid(y)
    o_ref[...] = y


def _seg_conv(proj, col0, w, b, *, act):
    bsz, tt, _ = proj.shape
    ch = w.shape[1]
    tm, tc = ROW_TILE, 1024
    cb0 = col0 // tc
    hb = tm // 8
    nhb = tt // 8
    return pl.pallas_call(
        functools.partial(_seg_conv_kernel, ctx_tiles=CTX_LEN // tm, act=act),
        grid=(bsz, tt // tm, ch // tc),
        in_specs=[pl.BlockSpec((None, tm, tc), lambda b_, t, c: (b_, t, cb0 + c)),
                  pl.BlockSpec((None, 8, tc), lambda b_, t, c: (b_, jnp.maximum(t * hb - 1, 0), cb0 + c)),
                  pl.BlockSpec((None, 8, tc), lambda b_, t, c: (b_, jnp.minimum((t + 1) * hb, nhb - 1), cb0 + c)),
                  pl.BlockSpec((3, tc), lambda b_, t, c: (0, c)),
                  pl.BlockSpec((1, tc), lambda b_, t, c: (0, c))],
        out_specs=pl.BlockSpec((None, tm, tc), lambda b_, t, c: (b_, t, c)),
        out_shape=jax.ShapeDtypeStruct((bsz, tt, ch), F32),
        compiler_params=_params(("arbitrary", "arbitrary", "arbitrary")),
        name="seg_conv",
    )(proj, proj, proj, w, b.reshape(1, ch))


def _pair_expand(v, h0):
    rows = v.shape[0]
    lane = lax.broadcasted_iota(jnp.int32, (rows, LANES), 1)
    lo = jnp.broadcast_to(v[:, h0:h0 + 1], (rows, LANES))
    hi = jnp.broadcast_to(v[:, h0 + 1:h0 + 2], (rows, LANES))
    return jnp.where(lane < SSM_HEAD_DIM, lo, hi)


def _ssd_kernel(xs_ref, b_ref, c_ref, dt_ref, bias_ref, alog_ref, o_ref, h_ref):
    d = pl.program_id(1)
    s = pl.program_id(2)
    q = SSM_CHUNK
    fwd = d == 0

    @pl.when(s == 0)
    def _():
        h_ref[...] = jnp.zeros_like(h_ref)

    li = lax.broadcasted_iota(jnp.int32, (q, q), 0)
    si = lax.broadcasted_iota(jnp.int32, (q, q), 1)
    sgn = jnp.where(fwd, 1, -1)
    incl = sgn * (si - li) <= 0
    tri = jnp.where(incl, 1.0, 0.0).astype(BF16)
    lane = lax.broadcasted_iota(jnp.int32, (q, LANES), 1)
    half = lane < SSM_HEAD_DIM

    dt = _softplus(dt_ref[...] + bias_ref[...])
    da = dt * (-jnp.exp(alog_ref[...]))
    cs = _sel_dot(tri, da, 3)
    cs_t = cs.T
    cs_end = jnp.where(fwd, cs[q - 1:q, :], cs[0:1, :])
    w_end = dt * jnp.exp(cs_end - cs)
    e_cs = jnp.exp(cs)
    c_dec = jnp.exp(cs_end)

    n_pairs = SSM_GROUPS * 4
    for g in range(SSM_GROUPS):
        bg = b_ref[:, g * SSM_STATE:(g + 1) * SSM_STATE]
        cg = c_ref[:, g * SSM_STATE:(g + 1) * SSM_STATE]
        cb = _bdot_nt(cg, bg)
        bg_t = bg.T
        for pp in range(n_pairs // SSM_GROUPS):
            p = g * 4 + pp
            h0 = 2 * p
            sl = slice(p * LANES, (p + 1) * LANES)
            xs = xs_ref[:, sl]
            xdt = xs * _pair_expand(dt, h0)
            h_old = h_ref[:, sl]
            y = _bdot(cg, h_old) * _pair_expand(e_cs, h0)
            for k in range(2):
                hd = h0 + k
                dl = jnp.broadcast_to(cs[:, hd:hd + 1], (q, q))
                ds_ = jnp.broadcast_to(cs_t[hd:hd + 1, :], (q, q))
                lm = jnp.exp(jnp.where(incl, dl - ds_, -1e30))
                msk = half if k == 0 else jnp.logical_not(half)
                y = y + _bdot(cb * lm, jnp.where(msk, xdt, 0.0))
            o_ref[:, sl] = y
            h_ref[:, sl] = h_old * _pair_expand(c_dec, h0) + _bdot(bg_t, xs * _pair_expand(w_end, h0))


def _ssd_scan(xbc, proj, dt_col, dt_bias, a_log):
    bsz, tt, _ = xbc.shape
    q = SSM_CHUNK
    nc = tt // q
    ncx = CTX_LEN // q
    d_ssm = xbc.shape[2] - 2 * SSM_GROUPS * SSM_STATE
    bc_w = SSM_GROUPS * SSM_STATE

    def chunk(d, s):
        rev = jnp.where(s < ncx, ncx - 1 - s, nc - 1 + ncx - s)
        return jnp.where(d == 0, s, rev)

    return pl.pallas_call(
        _ssd_kernel,
        grid=(bsz, 2, nc),
        in_specs=[pl.BlockSpec((None, q, d_ssm), lambda b, d, s: (b, chunk(d, s), 0)),
                  pl.BlockSpec((None, q, bc_w), lambda b, d, s: (b, chunk(d, s), d_ssm // bc_w)),
                  pl.BlockSpec((None, q, bc_w), lambda b, d, s: (b, chunk(d, s), d_ssm // bc_w + 1)),
                  pl.BlockSpec((None, q, LANES), lambda b, d, s: (b, chunk(d, s), dt_col // LANES)),
                  pl.BlockSpec((None, 1, LANES), lambda b, d, s: (d, 0, 0)),
                  pl.BlockSpec((None, 1, LANES), lambda b, d, s: (d, 0, 0))],
        out_specs=pl.BlockSpec((None, None, q, d_ssm), lambda b, d, s: (d, b, chunk(d, s), 0)),
        out_shape=jax.ShapeDtypeStruct((2, bsz, tt, d_ssm), F32),
        scratch_shapes=[pltpu.VMEM((SSM_STATE, d_ssm), F32)],
        compiler_params=_params(("arbitrary", "arbitrary", "arbitrary")),
        name="ssd_scan",
    )(xbc, xbc, xbc, proj, dt_bias, a_log)


def _stack2(x, half):
    return jnp.concatenate([jnp.where(half, x, 0.0), jnp.where(half, 0.0, x)], axis=0)


def _rwkv_kernel(r_ref, k_ref, v_ref, wdn_ref, adn_ref, w2_ref, a2_ref, w0_ref, a0_ref,
                 kk_ref, ka_ref, rk_ref, y_ref, bonus_ref, st_ref):
    d = pl.program_id(1)
    s = pl.program_id(2)
    fwd = d == 0
    c_len = RWKV_CHUNK
    n2 = 2 * c_len

    @pl.when(s == 0)
    def _():
        st_ref[...] = jnp.zeros_like(st_ref)

    ti = lax.broadcasted_iota(jnp.int32, (c_len, c_len), 0)
    ui = lax.broadcasted_iota(jnp.int32, (c_len, c_len), 1)
    sgn = jnp.where(fwd, 1, -1)
    tri = jnp.where(sgn * (ui - ti) <= 0, 1.0, 0.0).astype(BF16)
    ones2 = _half_ones()
    half = lax.broadcasted_iota(jnp.int32, (c_len, LANES), 1) < RWKV_HEAD

    rr = lax.broadcasted_iota(jnp.int32, (n2, n2), 0)
    cc = lax.broadcasted_iota(jnp.int32, (n2, n2), 1)
    same = (rr // c_len) == (cc // c_len)
    order = sgn * (cc % c_len - rr % c_len)
    strict = jnp.logical_and(same, order < 0)
    incl = jnp.logical_and(same, order <= 0)
    eye = rr == cc

    nb = r_ref.shape[0]
    n_pairs = r_ref.shape[2] // LANES
    tanh_w = [jnp.tanh(wdn_ref[bi]) for bi in range(nb)]
    adn = [adn_ref[bi] for bi in range(nb)]

    eye_f = jnp.where(eye, 1.0, 0.0)

    def prep(q, e):
        bi, p = divmod(q, n_pairs)
        sl = slice(p * LANES, (p + 1) * LANES)
        r = r_ref[bi, :, sl]
        k = k_ref[bi, :, sl]
        v = v_ref[bi, :, sl]
        z = w0_ref[:, sl] + _bdot(tanh_w[bi], w2_ref[:, sl])
        lw = -jnp.exp(-_softplus(-z) - 0.5)
        a_sig = _sigmoid(a0_ref[:, sl] + _bdot(adn[bi], a2_ref[:, sl]))
        kd = k * (1.0 + (a_sig - 1.0) * ka_ref[:, sl])
        kkv = k * kk_ref[:, sl]
        nrm = jnp.sqrt(_dot_sel(kkv * kkv, ones2, 2))
        kk = kkv / jnp.maximum(nrm, 1e-12)
        bvec = kk * a_sig
        bonus_ref[bi, :, sl] = _dot_sel(r * kd * rk_ref[:, sl], ones2, 2) * v

        c = _sel_dot(tri, lw, 3)
        e_nc = jnp.exp(-c)
        pe = jnp.exp(jnp.where(fwd, c[c_len - 1:c_len, :], c[0:1, :]))
        k_t = kd * e_nc
        b_t = bvec * e_nc
        e["at"] = _stack2(-kk * jnp.exp(c - lw), half).astype(BF16)
        e["rt"] = _stack2(r * jnp.exp(c), half)
        e["bk"] = jnp.concatenate([_stack2(b_t, half), _stack2(k_t, half)], axis=0).astype(BF16)
        e["v"] = _stack2(v, half).astype(BF16)
        e["kc"] = _stack2(k_t * pe, half).astype(BF16)
        e["bc"] = _stack2(b_t * pe, half).astype(BF16)
        e["pe"] = pe

    def products(p, e):
        lhs = jnp.concatenate([e["at"], e["rt"].astype(BF16)], axis=0)
        big = lax.dot_general(lhs, e["bk"], (((1,), (1,)), ((), ())), preferred_element_type=F32)
        a_ab = jnp.where(strict, big[0:n2, 0:n2], 0.0)
        e["a_k"] = jnp.concatenate([jnp.where(strict, big[0:n2, n2:2 * n2], 0.0),
                                    jnp.where(incl, big[n2:2 * n2, n2:2 * n2], 0.0)], axis=0).astype(BF16)
        e["a_rb"] = jnp.where(incl, big[n2:2 * n2, 0:n2], 0.0).astype(BF16)
        e["inv"] = eye_f + a_ab
        e["pw"] = a_ab.astype(BF16)

    def square(p, e):
        e["pw"] = jnp.dot(e["pw"], e["pw"], preferred_element_type=F32).astype(BF16)
        kv = jnp.dot(e["a_k"], e["v"], preferred_element_type=F32)
        e["akv"] = kv[0:n2].astype(BF16)
        e["rkv"] = kv[n2:2 * n2]

    def level(p, e):
        both = jnp.dot(jnp.concatenate([e["inv"].astype(BF16), e["pw"]], axis=0), e["pw"],
                       preferred_element_type=F32)
        e["inv"] = e["inv"] + both[0:n2]
        e["pw"] = both[n2:2 * n2].astype(BF16)

    def last_level(p, e):
        e["inv"] = (e["inv"] + jnp.dot(e["inv"].astype(BF16), e["pw"], preferred_element_type=F32)).astype(BF16)

    def solve(p, e):
        rhs = jnp.concatenate([e["at"], e["akv"]], axis=1)
        e["wu"] = jnp.dot(e["inv"], rhs, preferred_element_type=F32).astype(BF16)

    def apply(p, e):
        e["rb_wu"] = jnp.dot(e["a_rb"], e["wu"], preferred_element_type=F32)
        e["bc_wu"] = _bdot_tn(e["bc"], e["wu"])
        e["kc_v"] = _bdot_tn(e["kc"], e["v"])

    def state(q, e):
        bi, p = divmod(q, n_pairs)
        sl = slice(p * LANES, (p + 1) * LANES)
        r_w = e["rt"] + e["rb_wu"][:, 0:n2]
        g_m = jnp.where(eye, jnp.broadcast_to(e["pe"], (n2, n2)), 0.0) + e["bc_wu"][:, 0:n2]
        y0 = e["rkv"] + e["rb_wu"][:, n2:2 * n2]
        h_m = e["kc_v"] + e["bc_wu"][:, n2:2 * n2]
        st = st_ref[q]
        both = jnp.dot(jnp.concatenate([r_w, g_m], axis=0).astype(BF16), st.astype(BF16),
                       preferred_element_type=F32)
        ys = both[0:n2] + y0
        st_ref[q] = both[n2:2 * n2] + h_m
        y_ref[bi, :, sl] = ys[0:c_len, :] + ys[c_len:n2, :]

    stages = [prep, products, square, level, level, level, level, last_level, solve, apply, state]
    n_chains = nb * n_pairs
    env = [dict() for _ in range(n_chains)]
    for step in range((n_chains - 1) // RWKV_WAVE + len(stages)):
        for q in range(n_chains):
            if 0 <= step - q // RWKV_WAVE < len(stages):
                stages[step - q // RWKV_WAVE](q, env[q])


def _rwkv_scan(rkv, proj, wdn_col, adn_col, w2p, a2p, w0, a0, k_k, k_a, r_k):
    bsz, tt, d3 = rkv.shape
    dr = d3 // 3
    cl = RWKV_CHUNK
    nc = tt // cl
    ncx = CTX_LEN // cl

    def chunk(d, s):
        rev = jnp.where(s < ncx, ncx - 1 - s, nc - 1 + ncx - s)
        return jnp.where(d == 0, s, rev)

    nb = RWKV_BATCH if bsz % RWKV_BATCH == 0 else 1
    row = lambda: pl.BlockSpec((1, dr), lambda b, d, s: (0, 0))
    out_spec = pl.BlockSpec((None, nb, cl, dr), lambda b, d, s: (d, b, chunk(d, s), 0))
    return pl.pallas_call(
        _rwkv_kernel,
        grid=(bsz // nb, 2, nc),
        in_specs=[pl.BlockSpec((nb, cl, dr), lambda b, d, s: (b, chunk(d, s), 0)),
                  pl.BlockSpec((nb, cl, dr), lambda b, d, s: (b, chunk(d, s), 1)),
                  pl.BlockSpec((nb, cl, dr), lambda b, d, s: (b, chunk(d, s), 2)),
                  pl.BlockSpec((nb, cl, LANES), lambda b, d, s: (b, chunk(d, s), wdn_col // LANES)),
                  pl.BlockSpec((nb, cl, LANES), lambda b, d, s: (b, chunk(d, s), adn_col // LANES)),
                  pl.BlockSpec((None, LANES, dr), lambda b, d, s: (d, 0, 0)),
                  pl.BlockSpec((None, LANES, dr), lambda b, d, s: (d, 0, 0)),
                  pl.BlockSpec((None, 1, dr), lambda b, d, s: (d, 0, 0)),
                  pl.BlockSpec((None, 1, dr), lambda b, d, s: (d, 0, 0)),
                  row(), row(), row()],
        out_specs=[out_spec, out_spec],
        out_shape=[jax.ShapeDtypeStruct((2, bsz, tt, dr), F32)] * 2,
        scratch_shapes=[pltpu.VMEM((nb * dr // LANES, LANES, LANES), F32)],
        compiler_params=_params(("arbitrary", "arbitrary", "arbitrary")),
        name="rwkv_scan",
    )(rkv, rkv, rkv, proj, proj, w2p, a2p, w0, a0,
      k_k.reshape(1, dr), k_a.reshape(1, dr), r_k.reshape(1, dr))


def _merge_kernel(z_ref, xs_ref, yf_ref, yb_ref, r0_ref, r1_ref, b0_ref, b1_ref, gdn_ref, gate_ref,
                  mod_ref, dexp_ref, snorm_ref, sout_ref, lnw_ref, lnb_ref, g2_ref, rout_ref, wo_ref,
                  gpost_ref, o_ref):
    d_model = o_ref.shape[1]
    z = z_ref[...]
    y = dexp_ref[...] * xs_ref[...] + yf_ref[...] + yb_ref[...]
    y = y * (z * _sigmoid(z))
    gw = y.shape[1] // SSM_GROUPS
    yn = jnp.concatenate([_rms(y[:, g * gw:(g + 1) * gw]) for g in range(SSM_GROUPS)], axis=1)
    out_ssm = _bdot(yn * snorm_ref[...], sout_ref[...])

    yh = r0_ref[...] + r1_ref[...]
    ones2 = _half_ones()
    pieces = []
    for p in range(d_model // LANES):
        t = yh[:, p * LANES:(p + 1) * LANES]
        mu = _dot_sel(t, ones2, 2) * (1.0 / RWKV_HEAD)
        tc = t - mu
        var = _dot_sel(tc * tc, ones2, 2) * (1.0 / RWKV_HEAD)
        pieces.append(tc * lax.rsqrt(var + RWKV_LN_EPS))
    y_rw = jnp.concatenate(pieces, axis=1) * lnw_ref[...] + lnb_ref[...] + b0_ref[...] + b1_ref[...]
    g = _bdot(_sigmoid(gdn_ref[...]), g2_ref[...])
    out_rw = _bdot(y_rw * g, rout_ref[...])

    gate = _sigmoid(gate_ref[...])
    merged = gate[:, :d_model] * out_ssm + gate[:, d_model:] * out_rw
    yo = _bdot(merged, wo_ref[...])
    o_ref[...] = mod_ref[2:3, :] * (_rms(yo) * gpost_ref[...])


def _merge(proj, xbc, y_ssd, y_rw, bonus, mods, mod_row, cols, d_exp, ssm_norm, ssm_out,
           ln_w, ln_b, g2, rwkv_out, w_o, g_post):
    bsz, tt, _ = proj.shape
    d_ssm = y_ssd.shape[-1]
    d = y_rw.shape[-1]
    tm = ROW_TILE
    full = lambda shape: pl.BlockSpec(shape, lambda b, t: (0,) * len(shape))
    dir_spec = lambda di, w: pl.BlockSpec((None, None, tm, w), lambda b, t: (di, b, t, 0))
    return pl.pallas_call(
        _merge_kernel,
        grid=(bsz, tt // tm),
        in_specs=[pl.BlockSpec((None, tm, d_ssm), lambda b, t: (b, t, cols["z"] // d_ssm)),
                  pl.BlockSpec((None, tm, d_ssm), lambda b, t: (b, t, 0)),
                  dir_spec(0, d_ssm), dir_spec(1, d_ssm),
                  dir_spec(0, d), dir_spec(1, d), dir_spec(0, d), dir_spec(1, d),
                  pl.BlockSpec((None, tm, LANES), lambda b, t: (b, t, cols["g_dn"] // LANES)),
                  pl.BlockSpec((None, tm, 2 * d), lambda b, t: (b, t, cols["gate"] // (2 * d))),
                  pl.BlockSpec((None, 6, d), lambda b, t: (mod_row(b, t), 0, 0)),
                  full((1, d_ssm)), full((1, d_ssm)), full((d_ssm, d)),
                  full((1, d)), full((1, d)), full((LANES, d)), full((d, d)), full((d, d)), full((1, d))],
        out_specs=pl.BlockSpec((None, tm, d), lambda b, t: (b, t, 0)),
        out_shape=jax.ShapeDtypeStruct((bsz, tt, d), F32),
        compiler_params=_params(("arbitrary", "arbitrary")),
        name="merge",
    )(proj, xbc, y_ssd, y_ssd, y_rw, y_rw, bonus, bonus, proj, proj, mods,
      d_exp, ssm_norm.reshape(1, d_ssm), ssm_out, ln_w.reshape(1, d), ln_b.reshape(1, d), g2,
      rwkv_out, w_o, g_post.reshape(1, d))


def _ffn_up_kernel(x_ref, mix_ref, mod_ref, g_ref, w_ref, xo_ref, up_ref):
    xn = x_ref[...] + mix_ref[...]
    xo_ref[...] = xn
    h = _rms(xn) * g_ref[...]
    h = h * (1.0 + mod_ref[4:5, :]) + mod_ref[3:4, :]
    up_ref[...] = jnp.dot(h.astype(BF16), w_ref[...], preferred_element_type=F32)


def _ffn_up(x, mix, mix_row0, mods, mod_row, g, w):
    bsz, rows, d = x.shape
    n = w.shape[1]
    tm = ROW_TILE
    t0 = mix_row0 // tm
    return pl.pallas_call(
        _ffn_up_kernel,
        grid=(bsz, rows // tm),
        in_specs=[pl.BlockSpec((None, tm, d), lambda b, t: (b, t, 0)),
                  pl.BlockSpec((None, tm, d), lambda b, t: (b, t0 + t, 0)),
                  pl.BlockSpec((None, 6, d), lambda b, t: (mod_row(b, t), 0, 0)),
                  pl.BlockSpec((1, d), lambda b, t: (0, 0)),
                  pl.BlockSpec((d, n), lambda b, t: (0, 0))],
        out_specs=[pl.BlockSpec((None, tm, d), lambda b, t: (b, t, 0)),
                   pl.BlockSpec((None, tm, n), lambda b, t: (b, t, 0))],
        out_shape=[jax.ShapeDtypeStruct((bsz, rows, d), F32), jax.ShapeDtypeStruct((bsz, rows, n), F32)],
        compiler_params=_params(("arbitrary", "arbitrary")),
        name="ffn_up",
    )(x, mix, mods, g.reshape(1, d), w)


def _gelu_tanh(x):
    return 0.5 * x * (1.0 + jnp.tanh(0.7978845608028654 * (x + 0.044715 * x * x * x)))


def _ffn_down_kernel(gate_ref, val_ref, top_ref, bot_ref, cw_ref, cb_ref, wd_ref, x_ref, mod_ref, g_ref,
                     o_ref, act_ref, *, width, chunk):
    t = pl.program_id(1)
    nt = pl.num_programs(1)
    tm, ch = gate_ref.shape
    n_ext = tm + 2 * width
    col = lax.broadcasted_iota(jnp.int32, (n_ext, chunk), 0) % width
    top_on = jnp.where(t == 0, 0.0, 1.0)
    bot_on = jnp.where(t == nt - 1, 0.0, 1.0)
    for c in range(ch // chunk):
        sl = slice(c * chunk, (c + 1) * chunk)
        ext = jnp.concatenate([top_ref[:, sl] * top_on, gate_ref[:, sl], bot_ref[:, sl] * bot_on], axis=0)
        left = jnp.where(col == 0, 0.0, pltpu.roll(ext, 1, 0))
        right = jnp.where(col == width - 1, 0.0, pltpu.roll(ext, n_ext - 1, 0))
        acc = jnp.broadcast_to(cb_ref[:, sl], (tm, chunk))
        for dy in range(3):
            rs = slice(dy * width, dy * width + tm)
            acc = acc + cw_ref[3 * dy:3 * dy + 1, sl] * left[rs]
            acc = acc + cw_ref[3 * dy + 1:3 * dy + 2, sl] * ext[rs]
            acc = acc + cw_ref[3 * dy + 2:3 * dy + 3, sl] * right[rs]
        act_ref[:, sl] = (_gelu_tanh(acc) * val_ref[:, sl]).astype(BF16)
    f = jnp.dot(act_ref[...], wd_ref[...], preferred_element_type=F32)
    o_ref[...] = x_ref[...] + mod_ref[5:6, :] * (_rms(f) * g_ref[...])


def _ffn_down(up, x, mods, mod_row, conv_w, conv_b, w_down, g, *, width):
    bsz, rows, n2 = up.shape
    f = n2 // 2
    d = x.shape[2]
    tm = ROW_TILE
    rpt = tm // width
    nrast = rows // width
    full = lambda shape: pl.BlockSpec(shape, lambda b, t: (0,) * len(shape))
    return pl.pallas_call(
        functools.partial(_ffn_down_kernel, width=width, chunk=256),
        grid=(bsz, rows // tm),
        in_specs=[pl.BlockSpec((None, tm, f), lambda b, t: (b, t, 0)),
                  pl.BlockSpec((None, tm, f), lambda b, t: (b, t, 1)),
                  pl.BlockSpec((None, width, f), lambda b, t: (b, jnp.maximum(t * rpt - 1, 0), 0)),
                  pl.BlockSpec((None, width, f), lambda b, t: (b, jnp.minimum((t + 1) * rpt, nrast - 1), 0)),
                  full((9, f)), full((1, f)), full((f, d)),
                  pl.BlockSpec((None, tm, d), lambda b, t: (b, t, 0)),
                  pl.BlockSpec((None, 6, d), lambda b, t: (mod_row(b, t), 0, 0)),
                  full((1, d))],
        out_specs=pl.BlockSpec((None, tm, d), lambda b, t: (b, t, 0)),
        out_shape=jax.ShapeDtypeStruct((bsz, rows, d), F32),
        scratch_shapes=[pltpu.VMEM((tm, f), BF16)],
        compiler_params=_params(("arbitrary", "arbitrary")),
        name="ffn_down",
    )(up, up, up, up, conv_w.reshape(9, f), conv_b.reshape(1, f), w_down, x, mods, g.reshape(1, d))


def _grid_transpose(t, rows, cols):
    b = t.shape[0]
    rest = t.shape[2:]
    return jnp.swapaxes(t.reshape((b, rows, cols) + rest), 1, 2).reshape((b, rows * cols) + rest)


def kernel(x, c, ctx, c_ctx, ada_w, ada_b, norm_mix_pre, norm_mix_post, norm_ffn_pre, norm_ffn_post, w_in, ssm_conv_w, ssm_conv_b, ssm_dt_bias, ssm_a_log, ssm_d, ssm_norm, ssm_out, rwkv_conv_w, rwkv_w0, rwkv_w2, rwkv_a0, rwkv_a2, rwkv_g2, rwkv_k_k, rwkv_k_a, rwkv_r_k, rwkv_ln_w, rwkv_ln_b, rwkv_out, w_o, ffn_w_in, ffn_conv_w, ffn_conv_b, ffn_w_out):
    bsz, seq, d = x.shape
    depth = ada_w.shape[0]
    rows = seq // GRID_W
    ssm_heads = ssm_d.shape[1]
    d_ssm = ssm_heads * SSM_HEAD_DIM
    d_xbc = d_ssm + 2 * SSM_GROUPS * SSM_STATE
    lora_g = rwkv_g2.shape[1]
    ctx_tiles = CTX_LEN // ROW_TILE
    ctx_row = bsz

    cols = {"z": 0, "xbc": d_ssm, "rkv": d_ssm + d_xbc}
    cols["gate"] = cols["rkv"] + 3 * d
    cols["w_dn"] = cols["gate"] + 2 * d
    cols["a_dn"] = cols["w_dn"] + 2 * LORA
    cols["g_dn"] = cols["a_dn"] + 2 * LORA
    cols["dt"] = cols["g_dn"] + lora_g
    o_z, o_xbc, o_dt, o_rkv = 0, d_ssm, d_ssm + d_xbc, d_ssm + d_xbc + ssm_heads
    o_wdn = o_rkv + 3 * d
    o_adn = o_wdn + 2 * LORA
    o_gdn = o_adn + 2 * LORA
    o_gate = o_gdn + lora_g

    cond = jnp.concatenate([c, c_ctx[None, :], jnp.zeros((16 - bsz - 1, d), F32)], axis=0)
    seq_row = lambda b, t: jnp.where(t < ctx_tiles, ctx_row, b)
    lat_row = lambda b, t: b
    ctx_only_row = lambda b, t: ctx_row

    xl, xc = x, ctx
    for i in range(depth):
        last = i == depth - 1
        col_major = i % 2 == 1
        mods = _adaln(cond, ada_w[i], ada_b[i]).reshape(16, 6, d)

        wi = w_in[i]
        w_cat = jnp.concatenate(
            [wi[:, o_z:o_xbc], wi[:, o_xbc:o_dt], wi[:, o_rkv:o_wdn], wi[:, o_gate:o_gate + 2 * d],
             wi[:, o_wdn:o_adn], wi[:, o_adn:o_gdn], wi[:, o_gdn:o_gate], wi[:, o_dt:o_rkv],
             jnp.zeros((d, LANES - ssm_heads), F32)], axis=1).astype(BF16)

        xl_in = _grid_transpose(xl, rows, GRID_W) if col_major else xl
        x_cat = jnp.concatenate([xc, xl_in], axis=1)
        proj = _norm_proj(x_cat, mods, ctx_row, norm_mix_pre[i], w_cat, tn=w_cat.shape[1] // 7)

        xbc = _seg_conv(proj, cols["xbc"], ssm_conv_w[i], ssm_conv_b[i], act=True)
        rkv = _seg_conv(proj, cols["rkv"], rwkv_conv_w[i], jnp.zeros((3 * d,), F32), act=False)

        pad_heads = lambda a: jnp.pad(a, ((0, 0), (0, LANES - ssm_heads))).reshape(2, 1, LANES)
        y_ssd = _ssd_scan(xbc, proj, cols["dt"], pad_heads(ssm_dt_bias[i]), pad_heads(ssm_a_log[i]))

        zpad = jnp.zeros((LORA, d), F32)
        w2p = jnp.stack([jnp.concatenate([rwkv_w2[i, 0], zpad], 0), jnp.concatenate([zpad, rwkv_w2[i, 1]], 0)])
        a2p = jnp.stack([jnp.concatenate([rwkv_a2[i, 0], zpad], 0), jnp.concatenate([zpad, rwkv_a2[i, 1]], 0)])
        y_rw, bonus = _rwkv_scan(rkv, proj, cols["w_dn"], cols["a_dn"], w2p.astype(BF16), a2p.astype(BF16),
                                 rwkv_w0[i].reshape(2, 1, d), rwkv_a0[i].reshape(2, 1, d),
                                 rwkv_k_k[i], rwkv_k_a[i], rwkv_r_k[i])

        d_exp = jnp.repeat(ssm_d[i], SSM_HEAD_DIM).reshape(1, d_ssm)
        mix = _merge(proj, xbc, y_ssd, y_rw, bonus, mods, seq_row, cols, d_exp, ssm_norm[i],
                     ssm_out[i].astype(BF16), rwkv_ln_w[i], rwkv_ln_b[i], rwkv_g2[i].astype(BF16),
                     rwkv_out[i].astype(BF16), w_o[i].astype(BF16), norm_mix_post[i])

        w_up = ffn_w_in[i].astype(BF16)
        w_dn = ffn_w_out[i].astype(BF16)
        if col_major:
            xl, up = _ffn_up(xl, _grid_transpose(mix[:, CTX_LEN:], GRID_W, rows), 0, mods, lat_row,
                             norm_ffn_pre[i], w_up)
        else:
            xl, up = _ffn_up(xl, mix, CTX_LEN, mods, lat_row, norm_ffn_pre[i], w_up)
        xl = _ffn_down(up, xl, mods, lat_row, ffn_conv_w[i], ffn_conv_b[i], w_dn, norm_ffn_post[i], width=GRID_W)

        if not last:
            xc, up_c = _ffn_up(xc, mix, 0, mods, ctx_only_row, norm_ffn_pre[i], w_up)
            xc = _ffn_down(up_c, xc, mods, ctx_only_row, ffn_conv_w[i], ffn_conv_b[i], w_dn, norm_ffn_post[i],
                           width=CTX_LEN)
    return xl
```

```python
import functools

import jax
import jax.numpy as jnp
from jax import lax
from jax.experimental import pallas as pl
from jax.experimental.pallas import tpu as pltpu

F32 = jnp.float32
BF16 = jnp.bfloat16

GRID_W = 64
CTX_LEN = 256
NORM_EPS = 1e-6
RWKV_LN_EPS = 64e-5
LOG2_E = 1.4426950408889634

LANES = 128
SSM_HEAD_DIM = 64
SSM_GROUPS = 4
SSM_STATE = 128
SSM_CHUNK = 128
RWKV_HEAD = 64
RWKV_CHUNK = 64
RWKV_BATCH = 2
RWKV_WAVE = 2
LORA = 64

ROW_TILE = 256
VMEM_LIMIT = 56 * 1024 * 1024


def _params(sem):
    return pltpu.CompilerParams(dimension_semantics=sem, vmem_limit_bytes=VMEM_LIMIT)


def _bdot(a, b):
    return jnp.dot(a.astype(BF16), b.astype(BF16), preferred_element_type=F32)


def _bdot_nt(a, b):
    return lax.dot_general(a.astype(BF16), b.astype(BF16), (((1,), (1,)), ((), ())),
                           preferred_element_type=F32)


def _bdot_tn(a, b):
    return lax.dot_general(a.astype(BF16), b.astype(BF16), (((0,), (0,)), ((), ())),
                           preferred_element_type=F32)


def _split(x, parts):
    out = []
    r = x
    for _ in range(parts):
        p = r.astype(BF16)
        out.append(p)
        r = r - p.astype(F32)
    return out


def _sel_dot(sel, x, parts):
    acc = None
    for p in _split(x, parts):
        t = jnp.dot(sel, p, preferred_element_type=F32)
        acc = t if acc is None else acc + t
    return acc


def _dot_sel(x, sel, parts):
    acc = None
    for p in _split(x, parts):
        t = jnp.dot(p, sel, preferred_element_type=F32)
        acc = t if acc is None else acc + t
    return acc


def _sigmoid(x):
    return 1.0 / (1.0 + jnp.exp(-x))


def _softplus(x):
    return jnp.maximum(x, 0.0) + jnp.log(1.0 + jnp.exp(-jnp.abs(x)))


def _rms(x):
    return x * lax.rsqrt(jnp.mean(x * x, axis=-1, keepdims=True) + NORM_EPS)


def _head_ones(width):
    r = lax.broadcasted_iota(jnp.int32, (width, width), 0) // RWKV_HEAD
    c = lax.broadcasted_iota(jnp.int32, (width, width), 1) // RWKV_HEAD
    return (r == c).astype(BF16)


def _adaln_kernel(c_ref, w_ref, b_ref, o_ref):
    c = c_ref[...]
    s = c * _sigmoid(c)
    acc = None
    w = w_ref[...]
    w_parts = _split(w, 3)
    s_parts = _split(s, 3)
    for i in range(3):
        for j in range(3 - i):
            t = jnp.dot(s_parts[i], w_parts[j], preferred_element_type=F32)
            acc = t if acc is None else acc + t
    o_ref[...] = acc + b_ref[...]


def _adaln(cond, w, b):
    rows, d = cond.shape
    n = w.shape[1]
    tn = n // 12
    return pl.pallas_call(
        _adaln_kernel,
        grid=(n // tn,),
        in_specs=[pl.BlockSpec((rows, d), lambda j: (0, 0)),
                  pl.BlockSpec((d, tn), lambda j: (0, j)),
                  pl.BlockSpec((1, tn), lambda j: (0, j))],
        out_specs=pl.BlockSpec((rows, tn), lambda j: (0, j)),
        out_shape=jax.ShapeDtypeStruct((rows, n), F32),
        compiler_params=_params(("arbitrary",)),
        name="adaln",
    )(cond, w, b.reshape(1, n))


def _norm_proj_kernel(x_ref, modb_ref, modc_ref, g_ref, w_ref, o_ref, *, ctx_rows):
    x = x_ref[...]
    tm = x.shape[0]
    h = _rms(x) * g_ref[...]
    is_ctx = pl.program_id(2) * tm + lax.broadcasted_iota(jnp.int32, (tm, 1), 0) < ctx_rows
    shift = jnp.where(is_ctx, modc_ref[0:1, :], modb_ref[0:1, :])
    scale = jnp.where(is_ctx, modc_ref[1:2, :], modb_ref[1:2, :])
    h = h * (1.0 + scale) + shift
    o_ref[...] = jnp.dot(h.astype(BF16), w_ref[...], preferred_element_type=F32)


def _row_tile(rows, cap):
    return max(t for t in range(8, cap + 1, 8) if rows % t == 0)


def _norm_proj(x, mods, ctx_row, g, w, *, tn):
    bsz, rows, d = x.shape
    n = w.shape[1]
    tm = _row_tile(rows, 1088)
    return pl.pallas_call(
        functools.partial(_norm_proj_kernel, ctx_rows=CTX_LEN),
        grid=(n // tn, bsz, rows // tm),
        in_specs=[pl.BlockSpec((None, tm, d), lambda j, b, t: (b, t, 0)),
                  pl.BlockSpec((None, 6, d), lambda j, b, t: (b, 0, 0)),
                  pl.BlockSpec((None, 6, d), lambda j, b, t: (ctx_row, 0, 0)),
                  pl.BlockSpec((1, d), lambda j, b, t: (0, 0)),
                  pl.BlockSpec((d, tn), lambda j, b, t: (0, j))],
        out_specs=pl.BlockSpec((None, tm, tn), lambda j, b, t: (b, t, j)),
        out_shape=jax.ShapeDtypeStruct((bsz, rows, n), F32),
        compiler_params=_params(("arbitrary", "arbitrary", "arbitrary")),
        name="norm_proj",
    )(x, mods, mods, g.reshape(1, d), w)


def _scan_chunk(d, s, ncx, nc):
    rev = jnp.where(s < ncx, ncx - 1 - s, nc - 1 + ncx - s)
    return jnp.where(d == 0, s, rev)


def _conv3(x, prev_row, next_row, w, wcol):
    rows = x.shape[0]
    row = lax.broadcasted_iota(jnp.int32, x.shape, 0)
    xp = jnp.where(row == 0, prev_row, pltpu.roll(x, 1, 0))
    xn = jnp.where(row == rows - 1, next_row, pltpu.roll(x, rows - 1, 0))
    return w[0:1, wcol] * xp + w[1:2, wcol] * x + w[2:3, wcol] * xn


def _pair_expand(v, h0):
    rows = v.shape[0]
    lane = lax.broadcasted_iota(jnp.int32, (rows, LANES), 1)
    lo = jnp.broadcast_to(v[:, h0:h0 + 1], (rows, LANES))
    hi = jnp.broadcast_to(v[:, h0 + 1:h0 + 2], (rows, LANES))
    return jnp.where(lane < SSM_HEAD_DIM, lo, hi)


def _ssd_kernel(xs_ref, bc_ref, xsp_ref, xsn_ref, bcp_ref, bcn_ref, dt_ref, cw_ref, cbias_ref, bias_ref,
                alog_ref, dskip_ref, o_ref, h_ref, *, ncx):
    d = pl.program_id(1)
    s = pl.program_id(2)
    nc = pl.num_programs(2)
    q = SSM_CHUNK
    fwd = d == 0
    d_ssm = xs_ref.shape[1]
    bc_w = SSM_GROUPS * SSM_STATE

    @pl.when(s == 0)
    def _():
        h_ref[...] = jnp.zeros_like(h_ref)

    c = _scan_chunk(d, s, ncx, nc)
    seg_first = jnp.logical_or(c == 0, c == ncx)
    seg_last = jnp.logical_or(c == ncx - 1, c == nc - 1)

    def conv_act(x_ref, p_ref, n_ref, col, wcol):
        prev_row = jnp.where(seg_first, 0.0, p_ref[7:8, col])
        next_row = jnp.where(seg_last, 0.0, n_ref[0:1, col])
        y = _conv3(x_ref[:, col], prev_row, next_row, cw_ref, wcol) + cbias_ref[:, wcol]
        return y * _sigmoid(y)

    li = lax.broadcasted_iota(jnp.int32, (q, q), 0)
    si = lax.broadcasted_iota(jnp.int32, (q, q), 1)
    sgn = jnp.where(fwd, 1, -1)
    incl = sgn * (si - li) <= 0
    tri = jnp.where(incl, 1.0, 0.0).astype(BF16)
    half = lax.broadcasted_iota(jnp.int32, (q, LANES), 1) < SSM_HEAD_DIM

    dt = _softplus(dt_ref[...] + bias_ref[...])
    da = dt * (-LOG2_E * jnp.exp(alog_ref[...]))
    cs = _sel_dot(tri, da, 2)
    cs_end = jnp.where(fwd, cs[q - 1:q, :], cs[0:1, :])
    c_dec = jnp.exp2(cs_end)
    cs_t = (cs - jnp.log2(dt)).T
    wend_t = (dt * jnp.exp2(cs_end - cs)).T

    for g in range(SSM_GROUPS):
        gs = slice(g * SSM_STATE, (g + 1) * SSM_STATE)
        bg = conv_act(bc_ref, bcp_ref, bcn_ref, gs, slice(d_ssm + g * SSM_STATE, d_ssm + (g + 1) * SSM_STATE))
        cg = conv_act(bc_ref, bcp_ref, bcn_ref, slice(bc_w + g * SSM_STATE, bc_w + (g + 1) * SSM_STATE),
                      slice(d_ssm + bc_w + g * SSM_STATE, d_ssm + bc_w + (g + 1) * SSM_STATE))
        cb = _bdot_nt(cg, bg)
        bg_t = bg.T
        for pp in range(4):
            p = g * 4 + pp
            h0 = 2 * p
            sl = slice(p * LANES, (p + 1) * LANES)
            xs = conv_act(xs_ref, xsp_ref, xsn_ref, sl, sl)
            h_old = h_ref[:, sl]
            lhs_y, lhs_h = [], []
            for hd in (h0, h0 + 1):
                dl = jnp.broadcast_to(cs[:, hd:hd + 1], (q, q))
                ds_ = jnp.broadcast_to(cs_t[hd:hd + 1, :], (q, q))
                lm = jnp.exp2(jnp.where(incl, dl - ds_, -1e30))
                lhs_y.append(jnp.concatenate([cb * lm, cg * jnp.exp2(dl)], axis=1).astype(BF16))
                lhs_h.append((bg_t * jnp.broadcast_to(wend_t[hd:hd + 1, :], (q, q))).astype(BF16))
            yy = jnp.dot(jnp.concatenate(lhs_y, axis=0), jnp.concatenate([xs, h_old], axis=0).astype(BF16),
                         preferred_element_type=F32)
            o_ref[:, sl] = jnp.where(fwd, dskip_ref[:, sl], 0.0) * xs + jnp.where(half, yy[0:q], yy[q:2 * q])
            uu = jnp.dot(jnp.concatenate(lhs_h, axis=0), xs.astype(BF16), preferred_element_type=F32)
            h_ref[:, sl] = h_old * _pair_expand(c_dec, h0) + jnp.where(half, uu[0:q], uu[q:2 * q])


def _ssd_scan(proj, cols, conv_w, conv_b, dt_bias, a_log, d_skip):
    bsz, tt, _ = proj.shape
    q = SSM_CHUNK
    nc = tt // q
    ncx = CTX_LEN // q
    bc_w = SSM_GROUPS * SSM_STATE
    d_ssm = conv_w.shape[1] - 2 * bc_w
    xs_cb = cols["xbc"] // d_ssm
    bc_cb = (cols["xbc"] + d_ssm) // (2 * bc_w)
    hb = q // 8
    nhb = tt // 8
    ck = lambda d, s: _scan_chunk(d, s, ncx, nc)
    prev = lambda d, s: jnp.maximum(ck(d, s) * hb - 1, 0)
    nxt = lambda d, s: jnp.minimum((ck(d, s) + 1) * hb, nhb - 1)
    full = lambda shape: pl.BlockSpec(shape, lambda b, d, s: (0,) * len(shape))
    return pl.pallas_call(
        functools.partial(_ssd_kernel, ncx=ncx),
        grid=(bsz, 2, nc),
        in_specs=[pl.BlockSpec((None, q, d_ssm), lambda b, d, s: (b, ck(d, s), xs_cb)),
                  pl.BlockSpec((None, q, 2 * bc_w), lambda b, d, s: (b, ck(d, s), bc_cb)),
                  pl.BlockSpec((None, 8, d_ssm), lambda b, d, s: (b, prev(d, s), xs_cb)),
                  pl.BlockSpec((None, 8, d_ssm), lambda b, d, s: (b, nxt(d, s), xs_cb)),
                  pl.BlockSpec((None, 8, 2 * bc_w), lambda b, d, s: (b, prev(d, s), bc_cb)),
                  pl.BlockSpec((None, 8, 2 * bc_w), lambda b, d, s: (b, nxt(d, s), bc_cb)),
                  pl.BlockSpec((None, q, LANES), lambda b, d, s: (b, ck(d, s), cols["dt"] // LANES)),
                  full(conv_w.shape), full((1, conv_w.shape[1])),
                  pl.BlockSpec((None, 1, LANES), lambda b, d, s: (d, 0, 0)),
                  pl.BlockSpec((None, 1, LANES), lambda b, d, s: (d, 0, 0)),
                  full((1, d_ssm))],
        out_specs=pl.BlockSpec((None, None, q, d_ssm), lambda b, d, s: (d, b, ck(d, s), 0)),
        out_shape=jax.ShapeDtypeStruct((2, bsz, tt, d_ssm), F32),
        scratch_shapes=[pltpu.VMEM((SSM_STATE, d_ssm), F32)],
        compiler_params=_params(("arbitrary", "arbitrary", "arbitrary")),
        name="ssd_scan",
    )(proj, proj, proj, proj, proj, proj, proj, conv_w, conv_b.reshape(1, -1), dt_bias, a_log, d_skip)


def _rwkv_kernel(r_ref, k_ref, v_ref, rp_ref, kp_ref, vp_ref, rn_ref, kn_ref, vn_ref, wdn_ref, adn_ref,
                 cw_ref, w2_ref, a2_ref, w0_ref, a0_ref, kk_ref, ka_ref, rk_ref, y_ref, bonus_ref, st_ref, *, ncx):
    d = pl.program_id(1)
    s = pl.program_id(2)
    nc = pl.num_programs(2)
    fwd = d == 0
    c_len = RWKV_CHUNK
    n2 = 2 * c_len
    dr = r_ref.shape[2]

    @pl.when(s == 0)
    def _():
        st_ref[...] = jnp.zeros_like(st_ref)

    ck = _scan_chunk(d, s, ncx, nc)
    seg_first = jnp.logical_or(ck == 0, ck == ncx)
    seg_last = jnp.logical_or(ck == ncx - 1, ck == nc - 1)

    def conv(x_ref, p_ref, n_ref, bi, sl, which):
        prev_row = jnp.where(seg_first, 0.0, p_ref[bi, 7:8, sl])
        next_row = jnp.where(seg_last, 0.0, n_ref[bi, 0:1, sl])
        return _conv3(x_ref[bi, :, sl], prev_row, next_row, cw_ref,
                      slice(which * dr + sl.start, which * dr + sl.stop))

    qw = 2 * LANES
    ti = lax.broadcasted_iota(jnp.int32, (c_len, c_len), 0)
    ui = lax.broadcasted_iota(jnp.int32, (c_len, c_len), 1)
    sgn = jnp.where(fwd, 1, -1)
    tri = jnp.where(sgn * (ui - ti) <= 0, 1.0, 0.0).astype(BF16)
    ones4 = _head_ones(qw)
    even_head = (lax.broadcasted_iota(jnp.int32, (c_len, qw), 1) // RWKV_HEAD) % 2 == 0

    rr = lax.broadcasted_iota(jnp.int32, (n2, n2), 0)
    cc = lax.broadcasted_iota(jnp.int32, (n2, n2), 1)
    same = (rr // c_len) == (cc // c_len)
    order = sgn * (cc % c_len - rr % c_len)
    strict = jnp.logical_and(same, order < 0)
    incl = jnp.logical_and(same, order <= 0)
    eye_q = lax.broadcasted_iota(jnp.int32, (n2, qw), 0) == lax.broadcasted_iota(jnp.int32, (n2, qw), 1) % n2
    eye_f = jnp.where(eye_q, 1.0, 0.0)
    pair_lanes = (slice(0, LANES), slice(LANES, qw))

    nb = r_ref.shape[0]
    n_quads = r_ref.shape[2] // qw
    tanh_w = [jnp.tanh(wdn_ref[bi]) for bi in range(nb)]
    adn = [adn_ref[bi] for bi in range(nb)]

    def stack(x):
        return jnp.concatenate([jnp.where(even_head, x, 0.0), jnp.where(even_head, 0.0, x)], axis=0)

    def bdiag(m):
        z = jnp.zeros((m.shape[0], LANES), m.dtype)
        return jnp.concatenate([jnp.concatenate([m[:, 0:LANES], z], axis=1),
                                jnp.concatenate([z, m[:, LANES:qw]], axis=1)], axis=0)

    def lanes2(parts):
        return jnp.concatenate(parts, axis=1)

    def prep(q, e):
        bi, qd = divmod(q, n_quads)
        sl = slice(qd * qw, (qd + 1) * qw)
        r = conv(r_ref, rp_ref, rn_ref, bi, sl, 0)
        k = conv(k_ref, kp_ref, kn_ref, bi, sl, 1)
        v = conv(v_ref, vp_ref, vn_ref, bi, sl, 2)
        z = w0_ref[:, sl] + _bdot(tanh_w[bi], w2_ref[:, sl])
        lw = -jnp.exp(-_softplus(-z) - 0.5)
        a_sig = _sigmoid(a0_ref[:, sl] + _bdot(adn[bi], a2_ref[:, sl]))
        kd = k * (1.0 + (a_sig - 1.0) * ka_ref[:, sl])
        kkv = k * kk_ref[:, sl]
        nrm = jnp.sqrt(_dot_sel(kkv * kkv, ones4, 1))
        kk = kkv / jnp.maximum(nrm, 1e-12)
        bvec = kk * a_sig
        bonus_ref[bi, :, sl] = _dot_sel(r * kd * rk_ref[:, sl], ones4, 1) * v

        c = _sel_dot(tri, lw, 2)
        e_nc = jnp.exp(-c)
        pe = jnp.exp(jnp.where(fwd, c[c_len - 1:c_len, :], c[0:1, :]))
        k_t = kd * e_nc
        b_t = bvec * e_nc
        e["at"] = stack(-kk * jnp.exp(c - lw)).astype(BF16)
        e["rt"] = stack(r * jnp.exp(c))
        e["bk"] = jnp.concatenate([stack(b_t), stack(k_t)], axis=0).astype(BF16)
        e["v"] = stack(v).astype(BF16)
        e["kc"] = stack(k_t * pe).astype(BF16)
        e["bc"] = stack(b_t * pe).astype(BF16)
        e["pe"] = pe

    def products(q, e):
        rt_b = e["rt"].astype(BF16)
        a_ab, a_k, a_rb = [], [], []
        for lj in pair_lanes:
            lhs = jnp.concatenate([e["at"][:, lj], rt_b[:, lj]], axis=0)
            big = lax.dot_general(lhs, e["bk"][:, lj], (((1,), (1,)), ((), ())), preferred_element_type=F32)
            a_ab.append(jnp.where(strict, big[0:n2, 0:n2], 0.0))
            a_k.append(jnp.concatenate([jnp.where(strict, big[0:n2, n2:2 * n2], 0.0),
                                        jnp.where(incl, big[n2:2 * n2, n2:2 * n2], 0.0)], axis=0).astype(BF16))
            a_rb.append(jnp.where(incl, big[n2:2 * n2, 0:n2], 0.0).astype(BF16))
        a_ab = lanes2(a_ab)
        e["a_k"] = lanes2(a_k)
        e["a_rb"] = a_rb
        e["inv"] = eye_f + a_ab
        e["pw"] = a_ab.astype(BF16)

    def square(q, e):
        e["pw"] = jnp.dot(e["pw"], bdiag(e["pw"]), preferred_element_type=F32).astype(BF16)
        kv = jnp.dot(e["a_k"], bdiag(e["v"]), preferred_element_type=F32)
        e["akv"] = kv[0:n2].astype(BF16)
        e["rkv"] = kv[n2:2 * n2]

    def level(q, e):
        both = jnp.dot(jnp.concatenate([e["inv"].astype(BF16), e["pw"]], axis=0), bdiag(e["pw"]),
                       preferred_element_type=F32)
        e["inv"] = e["inv"] + both[0:n2]
        e["pw"] = both[n2:2 * n2].astype(BF16)

    def last_level(q, e):
        e["inv"] = (e["inv"] + jnp.dot(e["inv"].astype(BF16), bdiag(e["pw"]),
                                       preferred_element_type=F32)).astype(BF16)

    def solve(q, e):
        e["wu"] = [jnp.dot(e["inv"][:, lj], lanes2([e["at"][:, lj], e["akv"][:, lj]]),
                           preferred_element_type=F32).astype(BF16) for lj in pair_lanes]

    def apply(q, e):
        e["rb_wu"] = [jnp.dot(e["a_rb"][j], e["wu"][j], preferred_element_type=F32) for j in range(2)]
        e["bc_wu"] = [_bdot_tn(e["bc"][:, lj], e["wu"][j]) for j, lj in enumerate(pair_lanes)]
        e["kc_v"] = [_bdot_tn(e["kc"][:, lj], e["v"][:, lj]) for lj in pair_lanes]

    def state(q, e):
        bi, qd = divmod(q, n_quads)
        sl = slice(qd * qw, (qd + 1) * qw)
        r_w = e["rt"] + lanes2([t[:, 0:n2] for t in e["rb_wu"]])
        g_m = jnp.where(eye_q, jnp.broadcast_to(e["pe"], (n2, qw)), 0.0) + lanes2([t[:, 0:n2] for t in e["bc_wu"]])
        y0 = e["rkv"] + lanes2([t[:, n2:2 * n2] for t in e["rb_wu"]])
        h_m = lanes2(e["kc_v"]) + lanes2([t[:, n2:2 * n2] for t in e["bc_wu"]])
        st = st_ref[q]
        both = jnp.dot(jnp.concatenate([r_w, g_m], axis=0).astype(BF16), bdiag(st.astype(BF16)),
                       preferred_element_type=F32)
        ys = both[0:n2] + y0
        st_ref[q] = both[n2:2 * n2] + h_m
        y_ref[bi, :, sl] = ys[0:c_len, :] + ys[c_len:n2, :]

    stages = [prep, products, square, level, level, level, level, last_level, solve, apply, state]
    n_items = nb * n_quads
    env = [dict() for _ in range(n_items)]
    for step in range((n_items - 1) // RWKV_WAVE + len(stages)):
        for q in range(n_items):
            if 0 <= step - q // RWKV_WAVE < len(stages):
                stages[step - q // RWKV_WAVE](q, env[q])


def _rwkv_scan(proj, cols, conv_w, w2p, a2p, w0, a0, k_k, k_a, r_k):
    bsz, tt, _ = proj.shape
    dr = conv_w.shape[1] // 3
    cl = RWKV_CHUNK
    nc = tt // cl
    ncx = CTX_LEN // cl
    cb0 = cols["rkv"] // dr
    hb = cl // 8
    nhb = tt // 8
    ck = lambda d, s: _scan_chunk(d, s, ncx, nc)
    prev = lambda d, s: jnp.maximum(ck(d, s) * hb - 1, 0)
    nxt = lambda d, s: jnp.minimum((ck(d, s) + 1) * hb, nhb - 1)

    nb = RWKV_BATCH if bsz % RWKV_BATCH == 0 else 1
    row = lambda: pl.BlockSpec((1, dr), lambda b, d, s: (0, 0))
    main = lambda j: pl.BlockSpec((nb, cl, dr), lambda b, d, s: (b, ck(d, s), cb0 + j))
    halo_p = lambda j: pl.BlockSpec((nb, 8, dr), lambda b, d, s: (b, prev(d, s), cb0 + j))
    halo_n = lambda j: pl.BlockSpec((nb, 8, dr), lambda b, d, s: (b, nxt(d, s), cb0 + j))
    out_spec = pl.BlockSpec((None, nb, cl, dr), lambda b, d, s: (d, b, ck(d, s), 0))
    return pl.pallas_call(
        functools.partial(_rwkv_kernel, ncx=ncx),
        grid=(bsz // nb, 2, nc),
        in_specs=[main(0), main(1), main(2), halo_p(0), halo_p(1), halo_p(2), halo_n(0), halo_n(1), halo_n(2),
                  pl.BlockSpec((nb, cl, LANES), lambda b, d, s: (b, ck(d, s), cols["w_dn"] // LANES)),
                  pl.BlockSpec((nb, cl, LANES), lambda b, d, s: (b, ck(d, s), cols["a_dn"] // LANES)),
                  pl.BlockSpec(conv_w.shape, lambda b, d, s: (0, 0)),
                  pl.BlockSpec((None, LANES, dr), lambda b, d, s: (d, 0, 0)),
                  pl.BlockSpec((None, LANES, dr), lambda b, d, s: (d, 0, 0)),
                  pl.BlockSpec((None, 1, dr), lambda b, d, s: (d, 0, 0)),
                  pl.BlockSpec((None, 1, dr), lambda b, d, s: (d, 0, 0)),
                  row(), row(), row()],
        out_specs=[out_spec, out_spec],
        out_shape=[jax.ShapeDtypeStruct((2, bsz, tt, dr), F32)] * 2,
        scratch_shapes=[pltpu.VMEM((nb * dr // (2 * LANES), LANES, 2 * LANES), F32)],
        compiler_params=_params(("arbitrary", "arbitrary", "arbitrary")),
        name="rwkv_scan",
    )(*([proj] * 11), conv_w, w2p, a2p, w0, a0, k_k.reshape(1, dr), k_a.reshape(1, dr), r_k.reshape(1, dr))


def _merge_kernel(z_ref, yf_ref, yb_ref, r0_ref, r1_ref, b0_ref, b1_ref, gdn_ref, gate_ref,
                  mod_ref, snorm_ref, sout_ref, lnw_ref, lnb_ref, g2_ref, rout_ref, wo_ref, gpost_ref, o_ref):
    d_model = o_ref.shape[1]
    z = z_ref[...]
    y = yf_ref[...] + yb_ref[...]
    y = y * (z * _sigmoid(z))
    gw = y.shape[1] // SSM_GROUPS
    yn = jnp.concatenate([_rms(y[:, g * gw:(g + 1) * gw]) for g in range(SSM_GROUPS)], axis=1)
    out_ssm = _bdot(yn * snorm_ref[...], sout_ref[...])

    yh = r0_ref[...] + r1_ref[...]
    ones4 = _head_ones(2 * LANES)
    pieces = []
    for p in range(d_model // (2 * LANES)):
        t = yh[:, p * 2 * LANES:(p + 1) * 2 * LANES]
        mu = _dot_sel(t, ones4, 1) * (1.0 / RWKV_HEAD)
        tc = t - mu
        var = _dot_sel(tc * tc, ones4, 1) * (1.0 / RWKV_HEAD)
        pieces.append(tc * lax.rsqrt(var + RWKV_LN_EPS))
    y_rw = jnp.concatenate(pieces, axis=1) * lnw_ref[...] + lnb_ref[...] + b0_ref[...] + b1_ref[...]
    g = _bdot(_sigmoid(gdn_ref[...]), g2_ref[...])
    out_rw = _bdot(y_rw * g, rout_ref[...])

    gate = _sigmoid(gate_ref[...])
    merged = gate[:, :d_model] * out_ssm + gate[:, d_model:] * out_rw
    yo = _bdot(merged, wo_ref[...])
    o_ref[...] = mod_ref[2:3, :] * (_rms(yo) * gpost_ref[...])


def _merge(proj, y_ssd, y_rw, bonus, mods, mod_row, cols, ssm_norm, ssm_out, ln_w, ln_b, g2, rwkv_out, w_o, g_post):
    bsz, tt, _ = proj.shape
    d_ssm = y_ssd.shape[-1]
    d = y_rw.shape[-1]
    tm = ROW_TILE
    full = lambda shape: pl.BlockSpec(shape, lambda b, t: (0,) * len(shape))
    dir_spec = lambda di, w: pl.BlockSpec((None, None, tm, w), lambda b, t: (di, b, t, 0))
    return pl.pallas_call(
        _merge_kernel,
        grid=(bsz, tt // tm),
        in_specs=[pl.BlockSpec((None, tm, d_ssm), lambda b, t: (b, t, cols["z"] // d_ssm)),
                  dir_spec(0, d_ssm), dir_spec(1, d_ssm),
                  dir_spec(0, d), dir_spec(1, d), dir_spec(0, d), dir_spec(1, d),
                  pl.BlockSpec((None, tm, LANES), lambda b, t: (b, t, cols["g_dn"] // LANES)),
                  pl.BlockSpec((None, tm, 2 * d), lambda b, t: (b, t, cols["gate"] // (2 * d))),
                  pl.BlockSpec((None, 6, d), lambda b, t: (mod_row(b, t), 0, 0)),
                  full((1, d_ssm)), full((d_ssm, d)),
                  full((1, d)), full((1, d)), full((LANES, d)), full((d, d)), full((d, d)), full((1, d))],
        out_specs=pl.BlockSpec((None, tm, d), lambda b, t: (b, t, 0)),
        out_shape=jax.ShapeDtypeStruct((bsz, tt, d), F32),
        compiler_params=_params(("arbitrary", "arbitrary")),
        name="merge",
    )(proj, y_ssd, y_ssd, y_rw, y_rw, bonus, bonus, proj, proj, mods,
      ssm_norm.reshape(1, d_ssm), ssm_out, ln_w.reshape(1, d), ln_b.reshape(1, d), g2,
      rwkv_out, w_o, g_post.reshape(1, d))


def _ffn_up_kernel(x_ref, mix_ref, mod_ref, g_ref, w_ref, xo_ref, up_ref):
    xn = x_ref[...] + mix_ref[...]
    xo_ref[...] = xn
    h = _rms(xn) * g_ref[...]
    h = h * (1.0 + mod_ref[4:5, :]) + mod_ref[3:4, :]
    up_ref[...] = jnp.dot(h.astype(BF16), w_ref[...], preferred_element_type=F32)


def _ffn_up(x, mix, mix_row0, mods, mod_row, g, w):
    bsz, rows, d = x.shape
    n = w.shape[1]
    tm = ROW_TILE
    t0 = mix_row0 // tm
    return pl.pallas_call(
        _ffn_up_kernel,
        grid=(bsz, rows // tm),
        in_specs=[pl.BlockSpec((None, tm, d), lambda b, t: (b, t, 0)),
                  pl.BlockSpec((None, tm, d), lambda b, t: (b, t0 + t, 0)),
                  pl.BlockSpec((None, 6, d), lambda b, t: (mod_row(b, t), 0, 0)),
                  pl.BlockSpec((1, d), lambda b, t: (0, 0)),
                  pl.BlockSpec((d, n), lambda b, t: (0, 0))],
        out_specs=[pl.BlockSpec((None, tm, d), lambda b, t: (b, t, 0)),
                   pl.BlockSpec((None, tm, n), lambda b, t: (b, t, 0))],
        out_shape=[jax.ShapeDtypeStruct((bsz, rows, d), F32), jax.ShapeDtypeStruct((bsz, rows, n), F32)],
        compiler_params=_params(("arbitrary", "arbitrary")),
        name="ffn_up",
    )(x, mix, mods, g.reshape(1, d), w)


def _gelu_tanh(x):
    return 0.5 * x * (1.0 + jnp.tanh(0.7978845608028654 * (x + 0.044715 * x * x * x)))


def _ffn_down_kernel(gate_ref, val_ref, top_ref, bot_ref, cw_ref, cb_ref, wd_ref, x_ref, mod_ref, g_ref,
                     o_ref, act_ref, *, width, chunk):
    t = pl.program_id(1)
    nt = pl.num_programs(1)
    tm, ch = gate_ref.shape
    n_ext = tm + 2 * width
    col = lax.broadcasted_iota(jnp.int32, (n_ext, chunk), 0) % width
    top_on = jnp.where(t == 0, 0.0, 1.0)
    bot_on = jnp.where(t == nt - 1, 0.0, 1.0)
    for c in range(ch // chunk):
        sl = slice(c * chunk, (c + 1) * chunk)
        ext = jnp.concatenate([top_ref[:, sl] * top_on, gate_ref[:, sl], bot_ref[:, sl] * bot_on], axis=0)
        left = jnp.where(col == 0, 0.0, pltpu.roll(ext, 1, 0))
        right = jnp.where(col == width - 1, 0.0, pltpu.roll(ext, n_ext - 1, 0))
        acc = jnp.broadcast_to(cb_ref[:, sl], (tm, chunk))
        for dy in range(3):
            rs = slice(dy * width, dy * width + tm)
            acc = acc + cw_ref[3 * dy:3 * dy + 1, sl] * left[rs]
            acc = acc + cw_ref[3 * dy + 1:3 * dy + 2, sl] * ext[rs]
            acc = acc + cw_ref[3 * dy + 2:3 * dy + 3, sl] * right[rs]
        act_ref[:, sl] = (_gelu_tanh(acc) * val_ref[:, sl]).astype(BF16)
    f = jnp.dot(act_ref[...], wd_ref[...], preferred_element_type=F32)
    o_ref[...] = x_ref[...] + mod_ref[5:6, :] * (_rms(f) * g_ref[...])


def _ffn_down(up, x, mods, mod_row, conv_w, conv_b, w_down, g, *, width):
    bsz, rows, n2 = up.shape
    f = n2 // 2
    d = x.shape[2]
    tm = ROW_TILE
    rpt = tm // width
    nrast = rows // width
    full = lambda shape: pl.BlockSpec(shape, lambda b, t: (0,) * len(shape))
    return pl.pallas_call(
        functools.partial(_ffn_down_kernel, width=width, chunk=256),
        grid=(bsz, rows // tm),
        in_specs=[pl.BlockSpec((None, tm, f), lambda b, t: (b, t, 0)),
                  pl.BlockSpec((None, tm, f), lambda b, t: (b, t, 1)),
                  pl.BlockSpec((None, width, f), lambda b, t: (b, jnp.maximum(t * rpt - 1, 0), 0)),
                  pl.BlockSpec((None, width, f), lambda b, t: (b, jnp.minimum((t + 1) * rpt, nrast - 1), 0)),
                  full((9, f)), full((1, f)), full((f, d)),
                  pl.BlockSpec((None, tm, d), lambda b, t: (b, t, 0)),
                  pl.BlockSpec((None, 6, d), lambda b, t: (mod_row(b, t), 0, 0)),
                  full((1, d))],
        out_specs=pl.BlockSpec((None, tm, d), lambda b, t: (b, t, 0)),
        out_shape=jax.ShapeDtypeStruct((bsz, rows, d), F32),
        scratch_shapes=[pltpu.VMEM((tm, f), BF16)],
        compiler_params=_params(("arbitrary", "arbitrary")),
        name="ffn_down",
    )(up, up, up, up, conv_w.reshape(9, f), conv_b.reshape(1, f), w_down, x, mods, g.reshape(1, d))


def _grid_transpose(t, rows, cols):
    b = t.shape[0]
    rest = t.shape[2:]
    return jnp.swapaxes(t.reshape((b, rows, cols) + rest), 1, 2).reshape((b, rows * cols) + rest)


def kernel(x, c, ctx, c_ctx, ada_w, ada_b, norm_mix_pre, norm_mix_post, norm_ffn_pre, norm_ffn_post, w_in, ssm_conv_w, ssm_conv_b, ssm_dt_bias, ssm_a_log, ssm_d, ssm_norm, ssm_out, rwkv_conv_w, rwkv_w0, rwkv_w2, rwkv_a0, rwkv_a2, rwkv_g2, rwkv_k_k, rwkv_k_a, rwkv_r_k, rwkv_ln_w, rwkv_ln_b, rwkv_out, w_o, ffn_w_in, ffn_conv_w, ffn_conv_b, ffn_w_out):
    bsz, seq, d = x.shape
    depth = ada_w.shape[0]
    rows = seq // GRID_W
    ssm_heads = ssm_d.shape[1]
    d_ssm = ssm_heads * SSM_HEAD_DIM
    d_xbc = d_ssm + 2 * SSM_GROUPS * SSM_STATE
    lora_g = rwkv_g2.shape[1]
    ctx_tiles = CTX_LEN // ROW_TILE
    ctx_row = bsz

    cols = {"z": 0, "xbc": d_ssm, "rkv": d_ssm + d_xbc}
    cols["gate"] = cols["rkv"] + 3 * d
    cols["w_dn"] = cols["gate"] + 2 * d
    cols["a_dn"] = cols["w_dn"] + 2 * LORA
    cols["g_dn"] = cols["a_dn"] + 2 * LORA
    cols["dt"] = cols["g_dn"] + lora_g
    o_z, o_xbc, o_dt, o_rkv = 0, d_ssm, d_ssm + d_xbc, d_ssm + d_xbc + ssm_heads
    o_wdn = o_rkv + 3 * d
    o_adn = o_wdn + 2 * LORA
    o_gdn = o_adn + 2 * LORA
    o_gate = o_gdn + lora_g

    cond = jnp.concatenate([c, c_ctx[None, :], jnp.zeros((16 - bsz - 1, d), F32)], axis=0)
    seq_row = lambda b, t: jnp.where(t < ctx_tiles, ctx_row, b)
    lat_row = lambda b, t: b
    ctx_only_row = lambda b, t: ctx_row

    xl, xc = x, ctx
    for i in range(depth):
        last = i == depth - 1
        col_major = i % 2 == 1
        mods = _adaln(cond, ada_w[i], ada_b[i]).reshape(16, 6, d)

        wi = w_in[i]
        w_cat = jnp.concatenate(
            [wi[:, o_z:o_xbc], wi[:, o_xbc:o_dt], wi[:, o_rkv:o_wdn], wi[:, o_gate:o_gate + 2 * d],
             wi[:, o_wdn:o_adn], wi[:, o_adn:o_gdn], wi[:, o_gdn:o_gate], wi[:, o_dt:o_rkv],
             jnp.zeros((d, LANES - ssm_heads), F32)], axis=1).astype(BF16)

        xl_in = _grid_transpose(xl, rows, GRID_W) if col_major else xl
        x_cat = jnp.concatenate([xc, xl_in], axis=1)
        proj = _norm_proj(x_cat, mods, ctx_row, norm_mix_pre[i], w_cat, tn=w_cat.shape[1] // 7)

        pad_heads = lambda a: jnp.pad(a, ((0, 0), (0, LANES - ssm_heads))).reshape(2, 1, LANES)
        d_skip = jnp.repeat(ssm_d[i], SSM_HEAD_DIM).reshape(1, d_ssm)
        y_ssd = _ssd_scan(proj, cols, ssm_conv_w[i], ssm_conv_b[i], pad_heads(ssm_dt_bias[i]),
                          pad_heads(ssm_a_log[i]), d_skip)

        zpad = jnp.zeros((LORA, d), F32)
        w2p = jnp.stack([jnp.concatenate([rwkv_w2[i, 0], zpad], 0), jnp.concatenate([zpad, rwkv_w2[i, 1]], 0)])
        a2p = jnp.stack([jnp.concatenate([rwkv_a2[i, 0], zpad], 0), jnp.concatenate([zpad, rwkv_a2[i, 1]], 0)])
        y_rw, bonus = _rwkv_scan(proj, cols, rwkv_conv_w[i], w2p.astype(BF16), a2p.astype(BF16),
                                 rwkv_w0[i].reshape(2, 1, d), rwkv_a0[i].reshape(2, 1, d),
                                 rwkv_k_k[i], rwkv_k_a[i], rwkv_r_k[i])

        mix = _merge(proj, y_ssd, y_rw, bonus, mods, seq_row, cols, ssm_norm[i],
                     ssm_out[i].astype(BF16), rwkv_ln_w[i], rwkv_ln_b[i], rwkv_g2[i].astype(BF16),
                     rwkv_out[i].astype(BF16), w_o[i].astype(BF16), norm_mix_post[i])

        w_up = ffn_w_in[i].astype(BF16)
        w_dn = ffn_w_out[i].astype(BF16)
        if col_major:
            xl, up = _ffn_up(xl, _grid_transpose(mix[:, CTX_LEN:], GRID_W, rows), 0, mods, lat_row,
                             norm_ffn_pre[i], w_up)
        else:
            xl, up = _ffn_up(xl, mix, CTX_LEN, mods, lat_row, norm_ffn_pre[i], w_up)
        xl = _ffn_down(up, xl, mods, lat_row, ffn_conv_w[i], ffn_conv_b[i], w_dn, norm_ffn_post[i], width=GRID_W)

        if not last:
            xc, up_c = _ffn_up(xc, mix, 0, mods, ctx_only_row, norm_ffn_pre[i], w_up)
            xc = _ffn_down(up_c, xc, mods, ctx_only_row, ffn_conv_w[i], ffn_conv_b[i], w_dn, norm_ffn_post[i],
                           width=CTX_LEN)
    return xl
```

```python
import functools

import jax
import jax.numpy as jnp
from jax import lax
from jax.experimental import pallas as pl
from jax.experimental.pallas import tpu as pltpu

F32 = jnp.float32
BF16 = jnp.bfloat16

GRID_W = 64
CTX_LEN = 256
NORM_EPS = 1e-6
RWKV_LN_EPS = 64e-5
LOG2_E = 1.4426950408889634

LANES = 128
SSM_HEAD_DIM = 64
SSM_GROUPS = 4
SSM_STATE = 128
SSM_CHUNK = 128
RWKV_HEAD = 64
RWKV_CHUNK = 64
RWKV_BATCH = 4
RWKV_WAVE = 2
LORA = 64

ROW_TILE = 256
VMEM_LIMIT = 56 * 1024 * 1024


def _params(sem):
    return pltpu.CompilerParams(dimension_semantics=sem, vmem_limit_bytes=VMEM_LIMIT)


def _bdot(a, b):
    return jnp.dot(a.astype(BF16), b.astype(BF16), preferred_element_type=F32)


def _bdot_nt(a, b):
    return lax.dot_general(a.astype(BF16), b.astype(BF16), (((1,), (1,)), ((), ())),
                           preferred_element_type=F32)


def _split(x, parts):
    out = []
    r = x
    for _ in range(parts):
        p = r.astype(BF16)
        out.append(p)
        r = r - p.astype(F32)
    return out


def _sel_dot(sel, x, parts):
    acc = None
    for p in _split(x, parts):
        t = jnp.dot(sel, p, preferred_element_type=F32)
        acc = t if acc is None else acc + t
    return acc


def _dot_sel(x, sel, parts):
    acc = None
    for p in _split(x, parts):
        t = jnp.dot(p, sel, preferred_element_type=F32)
        acc = t if acc is None else acc + t
    return acc


def _sigmoid(x):
    return 1.0 / (1.0 + jnp.exp(-x))


def _softplus(x):
    return jnp.maximum(x, 0.0) + jnp.log(1.0 + jnp.exp(-jnp.abs(x)))


def _rms(x):
    return x * lax.rsqrt(jnp.mean(x * x, axis=-1, keepdims=True) + NORM_EPS)


def _head_ones(width):
    r = lax.broadcasted_iota(jnp.int32, (width, width), 0) // RWKV_HEAD
    c = lax.broadcasted_iota(jnp.int32, (width, width), 1) // RWKV_HEAD
    return (r == c).astype(BF16)


def _adaln_kernel(c_ref, w_ref, b_ref, o_ref):
    c = c_ref[...]
    s = c * _sigmoid(c)
    acc = None
    w = w_ref[...]
    w_parts = _split(w, 3)
    s_parts = _split(s, 3)
    for i in range(3):
        for j in range(3 - i):
            t = jnp.dot(s_parts[i], w_parts[j], preferred_element_type=F32)
            acc = t if acc is None else acc + t
    o_ref[...] = acc + b_ref[...]


def _adaln(cond, w, b):
    rows, d = cond.shape
    n = w.shape[1]
    tn = n // 12
    return pl.pallas_call(
        _adaln_kernel,
        grid=(n // tn,),
        in_specs=[pl.BlockSpec((rows, d), lambda j: (0, 0)),
                  pl.BlockSpec((d, tn), lambda j: (0, j)),
                  pl.BlockSpec((1, tn), lambda j: (0, j))],
        out_specs=pl.BlockSpec((rows, tn), lambda j: (0, j)),
        out_shape=jax.ShapeDtypeStruct((rows, n), F32),
        compiler_params=_params(("arbitrary",)),
        name="adaln",
    )(cond, w, b.reshape(1, n))


def _norm_proj_kernel(x_ref, modb_ref, modc_ref, g_ref, w_ref, o_ref, *, ctx_rows):
    x = x_ref[...]
    tm = x.shape[0]
    h = _rms(x) * g_ref[...]
    is_ctx = pl.program_id(2) * tm + lax.broadcasted_iota(jnp.int32, (tm, 1), 0) < ctx_rows
    shift = jnp.where(is_ctx, modc_ref[0:1, :], modb_ref[0:1, :])
    scale = jnp.where(is_ctx, modc_ref[1:2, :], modb_ref[1:2, :])
    h = h * (1.0 + scale) + shift
    o_ref[...] = jnp.dot(h.astype(BF16), w_ref[...], preferred_element_type=F32)


def _row_tile(rows, cap):
    return max(t for t in range(8, cap + 1, 8) if rows % t == 0)


def _norm_proj(x, mods, ctx_row, g, w, *, tn):
    bsz, rows, d = x.shape
    n = w.shape[1]
    tm = _row_tile(rows, 1088)
    return pl.pallas_call(
        functools.partial(_norm_proj_kernel, ctx_rows=CTX_LEN),
        grid=(n // tn, bsz, rows // tm),
        in_specs=[pl.BlockSpec((None, tm, d), lambda j, b, t: (b, t, 0)),
                  pl.BlockSpec((None, 6, d), lambda j, b, t: (b, 0, 0)),
                  pl.BlockSpec((None, 6, d), lambda j, b, t: (ctx_row, 0, 0)),
                  pl.BlockSpec((1, d), lambda j, b, t: (0, 0)),
                  pl.BlockSpec((d, tn), lambda j, b, t: (0, j))],
        out_specs=pl.BlockSpec((None, tm, tn), lambda j, b, t: (b, t, j)),
        out_shape=jax.ShapeDtypeStruct((bsz, rows, n), F32),
        compiler_params=_params(("arbitrary", "arbitrary", "arbitrary")),
        name="norm_proj",
    )(x, mods, mods, g.reshape(1, d), w)


def _scan_chunk(d, s, ncx, nc):
    rev = jnp.where(s < ncx, ncx - 1 - s, nc - 1 + ncx - s)
    return jnp.where(d == 0, s, rev)


def _conv3(x, prev_row, next_row, w, wcol):
    rows = x.shape[0]
    row = lax.broadcasted_iota(jnp.int32, x.shape, 0)
    xp = jnp.where(row == 0, prev_row, pltpu.roll(x, 1, 0))
    xn = jnp.where(row == rows - 1, next_row, pltpu.roll(x, rows - 1, 0))
    return w[0:1, wcol] * xp + w[1:2, wcol] * x + w[2:3, wcol] * xn


def _pair_expand(v, h0):
    rows = v.shape[0]
    lane = lax.broadcasted_iota(jnp.int32, (rows, LANES), 1)
    lo = jnp.broadcast_to(v[:, h0:h0 + 1], (rows, LANES))
    hi = jnp.broadcast_to(v[:, h0 + 1:h0 + 2], (rows, LANES))
    return jnp.where(lane < SSM_HEAD_DIM, lo, hi)


def _ssd_chunk(fwd, first_step, load_bc, load_xs, dt_ref, bias_ref, alog_ref, dskip_ref, o_ref, h_ref):
    q = SSM_CHUNK
    bc_w = SSM_GROUPS * SSM_STATE

    @pl.when(first_step)
    def _():
        h_ref[...] = jnp.zeros_like(h_ref)

    li = lax.broadcasted_iota(jnp.int32, (q, q), 0)
    si = lax.broadcasted_iota(jnp.int32, (q, q), 1)
    incl = si <= li if fwd else si >= li
    tri = jnp.where(incl, 1.0, 0.0).astype(BF16)
    half = lax.broadcasted_iota(jnp.int32, (q, LANES), 1) < SSM_HEAD_DIM

    dt = _softplus(dt_ref[...] + bias_ref[...])
    da = dt * (-LOG2_E * jnp.exp(alog_ref[...]))
    cs = _sel_dot(tri, da, 2)
    cs_end = cs[q - 1:q, :] if fwd else cs[0:1, :]
    c_dec = jnp.exp2(cs_end)
    cs_t = (cs - jnp.log2(dt)).T
    wend_t = (dt * jnp.exp2(cs_end - cs)).T

    for g in range(SSM_GROUPS):
        bg = load_bc(slice(g * SSM_STATE, (g + 1) * SSM_STATE))
        cg = load_bc(slice(bc_w + g * SSM_STATE, bc_w + (g + 1) * SSM_STATE))
        cb = _bdot_nt(cg, bg)
        bg_t = bg.T
        for pp in range(4):
            p = g * 4 + pp
            h0 = 2 * p
            sl = slice(p * LANES, (p + 1) * LANES)
            xs = load_xs(sl)
            h_old = h_ref[:, sl]
            lhs_y, lhs_h = [], []
            for hd in (h0, h0 + 1):
                dl = jnp.broadcast_to(cs[:, hd:hd + 1], (q, q))
                ds_ = jnp.broadcast_to(cs_t[hd:hd + 1, :], (q, q))
                lm = jnp.exp2(jnp.where(incl, dl - ds_, -1e30))
                lhs_y.append(jnp.concatenate([cb * lm, cg * jnp.exp2(dl)], axis=1).astype(BF16))
                lhs_h.append((bg_t * jnp.broadcast_to(wend_t[hd:hd + 1, :], (q, q))).astype(BF16))
            yy = jnp.dot(jnp.concatenate(lhs_y, axis=0), jnp.concatenate([xs, h_old], axis=0).astype(BF16),
                         preferred_element_type=F32)
            y = jnp.where(half, yy[0:q], yy[q:2 * q])
            o_ref[:, sl] = y if dskip_ref is None else y + dskip_ref[:, sl] * xs
            uu = jnp.dot(jnp.concatenate(lhs_h, axis=0), xs.astype(BF16), preferred_element_type=F32)
            h_ref[:, sl] = h_old * _pair_expand(c_dec, h0) + jnp.where(half, uu[0:q], uu[q:2 * q])


def _ssd_fwd_kernel(xs_ref, bc_ref, xsp_ref, xsn_ref, bcp_ref, bcn_ref, dt_ref, cw_ref, cbias_ref, bias_ref,
                    alog_ref, dskip_ref, o_ref, act_ref, h_ref, *, ncx):
    c = pl.program_id(1)
    nc = pl.num_programs(1)
    d_ssm = xs_ref.shape[1]
    seg_first = jnp.logical_or(c == 0, c == ncx)
    seg_last = jnp.logical_or(c == ncx - 1, c == nc - 1)

    def conv_act(x_ref, p_ref, n_ref, col, off):
        wcol = slice(off + col.start, off + col.stop)
        prev_row = jnp.where(seg_first, 0.0, p_ref[7:8, col])
        next_row = jnp.where(seg_last, 0.0, n_ref[0:1, col])
        y = _conv3(x_ref[:, col], prev_row, next_row, cw_ref, wcol) + cbias_ref[:, wcol]
        y = y * _sigmoid(y)
        act_ref[:, wcol] = y.astype(BF16)
        return y

    _ssd_chunk(True, c == 0, lambda col: conv_act(bc_ref, bcp_ref, bcn_ref, col, d_ssm),
               lambda col: conv_act(xs_ref, xsp_ref, xsn_ref, col, 0),
               dt_ref, bias_ref, alog_ref, dskip_ref, o_ref, h_ref)


def _ssd_bwd_kernel(xa_ref, bca_ref, dt_ref, bias_ref, alog_ref, o_ref, h_ref):
    _ssd_chunk(False, pl.program_id(1) == 0, lambda col: bca_ref[:, col].astype(F32),
               lambda col: xa_ref[:, col].astype(F32), dt_ref, bias_ref, alog_ref, None, o_ref, h_ref)


def _ssd_scan(proj, cols, conv_w, conv_b, dt_bias, a_log, d_skip):
    bsz, tt, _ = proj.shape
    q = SSM_CHUNK
    nc = tt // q
    ncx = CTX_LEN // q
    bc_w = SSM_GROUPS * SSM_STATE
    d_xbc = conv_w.shape[1]
    d_ssm = d_xbc - 2 * bc_w
    xs_cb = cols["xbc"] // d_ssm
    bc_cb = (cols["xbc"] + d_ssm) // (2 * bc_w)
    dt_cb = cols["dt"] // LANES
    hb = q // 8
    nhb = tt // 8
    full = lambda shape: pl.BlockSpec(shape, lambda b, s: (0,) * len(shape))
    head_row = lambda di: pl.BlockSpec((None, 1, LANES), lambda b, s: (di, 0, 0))
    prev = lambda s: jnp.maximum(s * hb - 1, 0)
    nxt = lambda s: jnp.minimum((s + 1) * hb, nhb - 1)
    y_fwd, act = pl.pallas_call(
        functools.partial(_ssd_fwd_kernel, ncx=ncx),
        grid=(bsz, nc),
        in_specs=[pl.BlockSpec((None, q, d_ssm), lambda b, s: (b, s, xs_cb)),
                  pl.BlockSpec((None, q, 2 * bc_w), lambda b, s: (b, s, bc_cb)),
                  pl.BlockSpec((None, 8, d_ssm), lambda b, s: (b, prev(s), xs_cb)),
                  pl.BlockSpec((None, 8, d_ssm), lambda b, s: (b, nxt(s), xs_cb)),
                  pl.BlockSpec((None, 8, 2 * bc_w), lambda b, s: (b, prev(s), bc_cb)),
                  pl.BlockSpec((None, 8, 2 * bc_w), lambda b, s: (b, nxt(s), bc_cb)),
                  pl.BlockSpec((None, q, LANES), lambda b, s: (b, s, dt_cb)),
                  full(conv_w.shape), full((1, d_xbc)), head_row(0), head_row(0), full((1, d_ssm))],
        out_specs=[pl.BlockSpec((None, q, d_ssm), lambda b, s: (b, s, 0)),
                   pl.BlockSpec((None, q, d_xbc), lambda b, s: (b, s, 0))],
        out_shape=[jax.ShapeDtypeStruct((bsz, tt, d_ssm), F32), jax.ShapeDtypeStruct((bsz, tt, d_xbc), BF16)],
        scratch_shapes=[pltpu.VMEM((SSM_STATE, d_ssm), F32)],
        compiler_params=_params(("arbitrary", "arbitrary")),
        name="ssd_fwd",
    )(proj, proj, proj, proj, proj, proj, proj, conv_w, conv_b.reshape(1, d_xbc), dt_bias, a_log, d_skip)

    ck = lambda s: _scan_chunk(1, s, ncx, nc)
    y_bwd = pl.pallas_call(
        _ssd_bwd_kernel,
        grid=(bsz, nc),
        in_specs=[pl.BlockSpec((None, q, d_ssm), lambda b, s: (b, ck(s), 0)),
                  pl.BlockSpec((None, q, 2 * bc_w), lambda b, s: (b, ck(s), d_ssm // (2 * bc_w))),
                  pl.BlockSpec((None, q, LANES), lambda b, s: (b, ck(s), dt_cb)),
                  head_row(1), head_row(1)],
        out_specs=pl.BlockSpec((None, q, d_ssm), lambda b, s: (b, ck(s), 0)),
        out_shape=jax.ShapeDtypeStruct((bsz, tt, d_ssm), F32),
        scratch_shapes=[pltpu.VMEM((SSM_STATE, d_ssm), F32)],
        compiler_params=_params(("arbitrary", "arbitrary")),
        name="ssd_bwd",
    )(act, act, proj, dt_bias, a_log)
    return y_fwd, y_bwd


def _rwkv_kernel(r_ref, k_ref, v_ref, rp_ref, kp_ref, vp_ref, rn_ref, kn_ref, vn_ref, wdn_ref, adn_ref,
                 cw_ref, w2_ref, a2_ref, w0_ref, a0_ref, kk_ref, ka_ref, rk_ref, y_ref, bonus_ref,
                 st_ref, nl_ref, nb_ref, *, ncx):
    d = pl.program_id(1)
    s = pl.program_id(2)
    nc = pl.num_programs(2)
    fwd = d == 0
    c_len = RWKV_CHUNK
    n2 = 2 * c_len
    dr = r_ref.shape[2]

    @pl.when(s == 0)
    def _():
        st_ref[...] = jnp.zeros_like(st_ref)
        nl_ref[...] = jnp.zeros_like(nl_ref)
        nb_ref[...] = jnp.zeros_like(nb_ref)

    ck = _scan_chunk(d, s, ncx, nc)
    seg_first = jnp.logical_or(ck == 0, ck == ncx)
    seg_last = jnp.logical_or(ck == ncx - 1, ck == nc - 1)

    def conv(x_ref, p_ref, n_ref, bi, sl, which):
        prev_row = jnp.where(seg_first, 0.0, p_ref[bi, 7:8, sl])
        next_row = jnp.where(seg_last, 0.0, n_ref[bi, 0:1, sl])
        return _conv3(x_ref[bi, :, sl], prev_row, next_row, cw_ref,
                      slice(which * dr + sl.start, which * dr + sl.stop))

    qw = 2 * LANES
    ti = lax.broadcasted_iota(jnp.int32, (c_len, c_len), 0)
    ui = lax.broadcasted_iota(jnp.int32, (c_len, c_len), 1)
    sgn = jnp.where(fwd, 1, -1)
    tri = jnp.where(sgn * (ui - ti) <= 0, 1.0, 0.0).astype(BF16)
    ones4 = _head_ones(qw)
    even_head = (lax.broadcasted_iota(jnp.int32, (c_len, qw), 1) // RWKV_HEAD) % 2 == 0

    rr = lax.broadcasted_iota(jnp.int32, (n2, n2), 0)
    cc = lax.broadcasted_iota(jnp.int32, (n2, n2), 1)
    same = (rr // c_len) == (cc // c_len)
    order = sgn * (cc % c_len - rr % c_len)
    strict = jnp.logical_and(same, order < 0)
    incl = jnp.logical_and(same, order <= 0)
    eye_q = lax.broadcasted_iota(jnp.int32, (n2, qw), 0) == lax.broadcasted_iota(jnp.int32, (n2, qw), 1) % n2
    eye_f = jnp.where(eye_q, 1.0, 0.0)
    pair_lanes = (slice(0, LANES), slice(LANES, qw))

    nb = r_ref.shape[0]
    n_quads = r_ref.shape[2] // qw
    tanh_w = [jnp.tanh(wdn_ref[bi]) for bi in range(nb)]
    adn = [adn_ref[bi] for bi in range(nb)]

    def stack(x):
        return jnp.concatenate([jnp.where(even_head, x, 0.0), jnp.where(even_head, 0.0, x)], axis=0)

    def bdiag(m):
        z = jnp.zeros((m.shape[0], LANES), m.dtype)
        return jnp.concatenate([jnp.concatenate([m[:, 0:LANES], z], axis=1),
                                jnp.concatenate([z, m[:, LANES:qw]], axis=1)], axis=0)

    def lanes2(parts):
        return jnp.concatenate(parts, axis=1)

    def pair_transposed(m):
        return lanes2([m[:, lj].T for lj in pair_lanes])

    def prep(q, e):
        bi, qd = divmod(q, n_quads)
        sl = slice(qd * qw, (qd + 1) * qw)
        r = conv(r_ref, rp_ref, rn_ref, bi, sl, 0)
        k = conv(k_ref, kp_ref, kn_ref, bi, sl, 1)
        v = conv(v_ref, vp_ref, vn_ref, bi, sl, 2)
        z = w0_ref[:, sl] + _bdot(tanh_w[bi], w2_ref[:, sl])
        lw = -jnp.exp(-_softplus(-z) - 0.5)
        a_sig = _sigmoid(a0_ref[:, sl] + _bdot(adn[bi], a2_ref[:, sl]))
        kd = k * (1.0 + (a_sig - 1.0) * ka_ref[:, sl])
        kkv = k * kk_ref[:, sl]
        nrm = jnp.sqrt(_dot_sel(kkv * kkv, ones4, 1))
        kk = kkv / jnp.maximum(nrm, 1e-12)
        bvec = kk * a_sig
        bonus_ref[bi, :, sl] = _dot_sel(r * kd * rk_ref[:, sl], ones4, 1) * v

        c = _sel_dot(tri, lw, 2)
        e_nc = jnp.exp(-c)
        pe = jnp.exp(jnp.where(fwd, c[c_len - 1:c_len, :], c[0:1, :]))
        k_t = kd * e_nc
        b_t = bvec * e_nc
        e["at"] = stack(-kk * jnp.exp(c - lw)).astype(BF16)
        e["rt"] = stack(r * jnp.exp(c))
        e["bk"] = jnp.concatenate([stack(b_t), stack(k_t)], axis=0).astype(BF16)
        e["v"] = stack(v).astype(BF16)
        e["kc"] = pair_transposed(stack(k_t * pe)).astype(BF16)
        e["bc"] = pair_transposed(stack(b_t * pe)).astype(BF16)
        e["pe"] = pe

    def products(q, e):
        rt_b = e["rt"].astype(BF16)
        a_ab, a_k, a_rb = [], [], []
        for lj in pair_lanes:
            lhs = jnp.concatenate([e["at"][:, lj], rt_b[:, lj]], axis=0)
            big = lax.dot_general(lhs, e["bk"][:, lj], (((1,), (1,)), ((), ())), preferred_element_type=F32)
            a_ab.append(jnp.where(strict, big[0:n2, 0:n2], 0.0))
            a_k.append(jnp.concatenate([jnp.where(strict, big[0:n2, n2:2 * n2], 0.0),
                                        jnp.where(incl, big[n2:2 * n2, n2:2 * n2], 0.0)], axis=0).astype(BF16))
            a_rb.append(jnp.where(incl, big[n2:2 * n2, 0:n2], 0.0).astype(BF16))
        a_ab = lanes2(a_ab)
        e["a_k"] = lanes2(a_k)
        e["a_rb"] = a_rb
        e["inv"] = eye_f + a_ab
        put_power(q, e["inv"], a_ab.astype(BF16))

    def put_power(q, inv, pw):
        if inv is not None:
            nl_ref[q, 0:n2, :] = inv.astype(BF16)
        nl_ref[q, n2:2 * n2, :] = pw
        nb_ref[q, 0:n2, 0:LANES] = pw[:, 0:LANES]
        nb_ref[q, n2:2 * n2, LANES:qw] = pw[:, LANES:qw]

    def square(q, e):
        pw = jnp.dot(nl_ref[q, n2:2 * n2, :], nb_ref[q], preferred_element_type=F32).astype(BF16)
        put_power(q, None, pw)
        kv =jnp.dot(jnp.concatenate([e["a_k"], e["kc"]], axis=0), bdiag(e["v"]),
                     preferred_element_type=F32)
        e["akv"] = kv[0:n2].astype(BF16)
        e["rkv"] = kv[n2:2 * n2]
        e["kc_v"] = kv[2 * n2:3 * n2]

    def level(q, e):
        both = jnp.dot(nl_ref[q], nb_ref[q], preferred_element_type=F32)
        e["inv"] = e["inv"] + both[0:n2]
        put_power(q, e["inv"], both[n2:2 * n2].astype(BF16))

    def last_level(q, e):
        e["inv"] = (e["inv"] + jnp.dot(nl_ref[q, 0:n2, :], nb_ref[q],
                                       preferred_element_type=F32)).astype(BF16)

    def solve(q, e):
        e["wu"] = [jnp.dot(e["inv"][:, lj], lanes2([e["at"][:, lj], e["akv"][:, lj]]),
                           preferred_element_type=F32).astype(BF16) for lj in pair_lanes]

    def apply(q, e):
        both = [jnp.dot(jnp.concatenate([e["a_rb"][j], e["bc"][:, lj]], axis=0), e["wu"][j],
                        preferred_element_type=F32) for j, lj in enumerate(pair_lanes)]
        e["rb_wu"] = [t[0:n2] for t in both]
        e["bc_wu"] = [t[n2:2 * n2] for t in both]

    def state(q, e):
        bi, qd = divmod(q, n_quads)
        sl = slice(qd * qw, (qd + 1) * qw)
        r_w = e["rt"] + lanes2([t[:, 0:n2] for t in e["rb_wu"]])
        g_m = jnp.where(eye_q, jnp.broadcast_to(e["pe"], (n2, qw)), 0.0) + lanes2([t[:, 0:n2] for t in e["bc_wu"]])
        y0 = e["rkv"] + lanes2([t[:, n2:2 * n2] for t in e["rb_wu"]])
        h_m = e["kc_v"] + lanes2([t[:, n2:2 * n2] for t in e["bc_wu"]])
        st = st_ref[q]
        both = jnp.dot(jnp.concatenate([r_w, g_m], axis=0).astype(BF16), bdiag(st.astype(BF16)),
                       preferred_element_type=F32)
        ys = both[0:n2] + y0
        st_ref[q] = both[n2:2 * n2] + h_m
        y_ref[bi, :, sl] = ys[0:c_len, :] + ys[c_len:n2, :]

    stages = [prep, products, square, level, level, level, level, last_level, solve, apply, state]
    n_items = nb * n_quads
    env = [dict() for _ in range(n_items)]
    for step in range((n_items - 1) // RWKV_WAVE + len(stages)):
        for q in range(n_items):
            if 0 <= step - q // RWKV_WAVE < len(stages):
                stages[step - q // RWKV_WAVE](q, env[q])


def _rwkv_scan(proj, cols, conv_w, w2p, a2p, w0, a0, k_k, k_a, r_k):
    bsz, tt, _ = proj.shape
    dr = conv_w.shape[1] // 3
    cl = RWKV_CHUNK
    nc = tt // cl
    ncx = CTX_LEN // cl
    cb0 = cols["rkv"] // dr
    hb = cl // 8
    nhb = tt // 8
    ck = lambda d, s: _scan_chunk(d, s, ncx, nc)
    prev = lambda d, s: jnp.maximum(ck(d, s) * hb - 1, 0)
    nxt = lambda d, s: jnp.minimum((ck(d, s) + 1) * hb, nhb - 1)

    nb = RWKV_BATCH if bsz % RWKV_BATCH == 0 else 1
    n_items = nb * dr // (2 * LANES)
    row = lambda: pl.BlockSpec((1, dr), lambda b, d, s: (0, 0))
    main = lambda j: pl.BlockSpec((nb, cl, dr), lambda b, d, s: (b, ck(d, s), cb0 + j))
    halo_p = lambda j: pl.BlockSpec((nb, 8, dr), lambda b, d, s: (b, prev(d, s), cb0 + j))
    halo_n = lambda j: pl.BlockSpec((nb, 8, dr), lambda b, d, s: (b, nxt(d, s), cb0 + j))
    out_spec = pl.BlockSpec((None, nb, cl, dr), lambda b, d, s: (d, b, ck(d, s), 0))
    return pl.pallas_call(
        functools.partial(_rwkv_kernel, ncx=ncx),
        grid=(bsz // nb, 2, nc),
        in_specs=[main(0), main(1), main(2), halo_p(0), halo_p(1), halo_p(2), halo_n(0), halo_n(1), halo_n(2),
                  pl.BlockSpec((nb, cl, LANES), lambda b, d, s: (b, ck(d, s), cols["w_dn"] // LANES)),
                  pl.BlockSpec((nb, cl, LANES), lambda b, d, s: (b, ck(d, s), cols["a_dn"] // LANES)),
                  pl.BlockSpec(conv_w.shape, lambda b, d, s: (0, 0)),
                  pl.BlockSpec((None, LANES, dr), lambda b, d, s: (d, 0, 0)),
                  pl.BlockSpec((None, LANES, dr), lambda b, d, s: (d, 0, 0)),
                  pl.BlockSpec((None, 1, dr), lambda b, d, s: (d, 0, 0)),
                  pl.BlockSpec((None, 1, dr), lambda b, d, s: (d, 0, 0)),
                  row(), row(), row()],
        out_specs=[out_spec, out_spec],
        out_shape=[jax.ShapeDtypeStruct((2, bsz, tt, dr), F32)] * 2,
        scratch_shapes=[pltpu.VMEM((n_items, LANES, 2 * LANES), F32),
                        pltpu.VMEM((n_items, 2 * LANES, 2 * LANES), BF16),
                        pltpu.VMEM((n_items, 2 * LANES, 2 * LANES), BF16)],
        compiler_params=_params(("arbitrary", "arbitrary", "arbitrary")),
        name="rwkv_scan",
    )(*([proj] * 11), conv_w, w2p, a2p, w0, a0, k_k.reshape(1, dr), k_a.reshape(1, dr), r_k.reshape(1, dr))


def _merge_kernel(z_ref, yf_ref, yb_ref, r0_ref, r1_ref, b0_ref, b1_ref, gdn_ref, gate_ref,
                  mod_ref, snorm_ref, sout_ref, lnw_ref, lnb_ref, g2_ref, rout_ref, wo_ref, gpost_ref, o_ref):
    d_model = o_ref.shape[1]
    z = z_ref[...]
    y = yf_ref[...] + yb_ref[...]
    y = y * (z * _sigmoid(z))
    gw = y.shape[1] // SSM_GROUPS
    yn = jnp.concatenate([_rms(y[:, g * gw:(g + 1) * gw]) for g in range(SSM_GROUPS)], axis=1)
    out_ssm = _bdot(yn * snorm_ref[...], sout_ref[...])

    yh = r0_ref[...] + r1_ref[...]
    ones4 = _head_ones(2 * LANES)
    pieces = []
    for p in range(d_model // (2 * LANES)):
        t = yh[:, p * 2 * LANES:(p + 1) * 2 * LANES]
        mu = _dot_sel(t, ones4, 1) * (1.0 / RWKV_HEAD)
        tc = t - mu
        var = _dot_sel(tc * tc, ones4, 1) * (1.0 / RWKV_HEAD)
        pieces.append(tc * lax.rsqrt(var + RWKV_LN_EPS))
    y_rw = jnp.concatenate(pieces, axis=1) * lnw_ref[...] + lnb_ref[...] + b0_ref[...] + b1_ref[...]
    g = _bdot(_sigmoid(gdn_ref[...]), g2_ref[...])
    out_rw = _bdot(y_rw * g, rout_ref[...])

    gate = _sigmoid(gate_ref[...])
    merged = gate[:, :d_model] * out_ssm + gate[:, d_model:] * out_rw
    yo = _bdot(merged, wo_ref[...])
    o_ref[...] = mod_ref[2:3, :] * (_rms(yo) * gpost_ref[...])


def _merge(proj, y_ssd_f, y_ssd_b, y_rw, bonus, mods, mod_row, cols, ssm_norm, ssm_out, ln_w, ln_b, g2, rwkv_out,
           w_o, g_post):
    bsz, tt, _ = proj.shape
    d_ssm = y_ssd_f.shape[-1]
    d = y_rw.shape[-1]
    tm = ROW_TILE
    full = lambda shape: pl.BlockSpec(shape, lambda b, t: (0,) * len(shape))
    dir_spec = lambda di, w: pl.BlockSpec((None, None, tm, w), lambda b, t: (di, b, t, 0))
    row_spec = lambda w: pl.BlockSpec((None, tm, w), lambda b, t: (b, t, 0))
    return pl.pallas_call(
        _merge_kernel,
        grid=(bsz, tt // tm),
        in_specs=[pl.BlockSpec((None, tm, d_ssm), lambda b, t: (b, t, cols["z"] // d_ssm)),
                  row_spec(d_ssm), row_spec(d_ssm),
                  dir_spec(0, d), dir_spec(1, d), dir_spec(0, d), dir_spec(1, d),
                  pl.BlockSpec((None, tm, LANES), lambda b, t: (b, t, cols["g_dn"] // LANES)),
                  pl.BlockSpec((None, tm, 2 * d), lambda b, t: (b, t, cols["gate"] // (2 * d))),
                  pl.BlockSpec((None, 6, d), lambda b, t: (mod_row(b, t), 0, 0)),
                  full((1, d_ssm)), full((d_ssm, d)),
                  full((1, d)), full((1, d)), full((LANES, d)), full((d, d)), full((d, d)), full((1, d))],
        out_specs=pl.BlockSpec((None, tm, d), lambda b, t: (b, t, 0)),
        out_shape=jax.ShapeDtypeStruct((bsz, tt, d), F32),
        compiler_params=_params(("arbitrary", "arbitrary")),
        name="merge",
    )(proj, y_ssd_f, y_ssd_b, y_rw, y_rw, bonus, bonus, proj, proj, mods,
      ssm_norm.reshape(1, d_ssm), ssm_out, ln_w.reshape(1, d), ln_b.reshape(1, d), g2,
      rwkv_out, w_o, g_post.reshape(1, d))


def _ffn_up_kernel(x_ref, mix_ref, mod_ref, g_ref, w_ref, xo_ref, up_ref):
    xn = x_ref[...] + mix_ref[...]
    xo_ref[...] = xn
    h = _rms(xn) * g_ref[...]
    h = h * (1.0 + mod_ref[4:5, :]) + mod_ref[3:4, :]
    up_ref[...] = jnp.dot(h.astype(BF16), w_ref[...], preferred_element_type=F32)


def _ffn_up(x, mix, mix_row0, mods, mod_row, g, w):
    bsz, rows, d = x.shape
    n = w.shape[1]
    tm = ROW_TILE
    t0 = mix_row0 // tm
    return pl.pallas_call(
        _ffn_up_kernel,
        grid=(bsz, rows // tm),
        in_specs=[pl.BlockSpec((None, tm, d), lambda b, t: (b, t, 0)),
                  pl.BlockSpec((None, tm, d), lambda b, t: (b, t0 + t, 0)),
                  pl.BlockSpec((None, 6, d), lambda b, t: (mod_row(b, t), 0, 0)),
                  pl.BlockSpec((1, d), lambda b, t: (0, 0)),
                  pl.BlockSpec((d, n), lambda b, t: (0, 0))],
        out_specs=[pl.BlockSpec((None, tm, d), lambda b, t: (b, t, 0)),
                   pl.BlockSpec((None, tm, n), lambda b, t: (b, t, 0))],
        out_shape=[jax.ShapeDtypeStruct((bsz, rows, d), F32), jax.ShapeDtypeStruct((bsz, rows, n), F32)],
        compiler_params=_params(("arbitrary", "arbitrary")),
        name="ffn_up",
    )(x, mix, mods, g.reshape(1, d), w)


def _gelu_tanh(x):
    return 0.5 * x * (1.0 + jnp.tanh(0.7978845608028654 * (x + 0.044715 * x * x * x)))


def _ffn_down_kernel(gate_ref, val_ref, top_ref, bot_ref, cw_ref, cb_ref, wd_ref, x_ref, mod_ref, g_ref,
                     o_ref, act_ref, *, width, chunk):
    t = pl.program_id(1)
    nt = pl.num_programs(1)
    tm, ch = gate_ref.shape
    n_ext = tm + 2 * width
    col = lax.broadcasted_iota(jnp.int32, (n_ext, chunk), 0) % width
    top_on = jnp.where(t == 0, 0.0, 1.0)
    bot_on = jnp.where(t == nt - 1, 0.0, 1.0)
    for c in range(ch // chunk):
        sl = slice(c * chunk, (c + 1) * chunk)
        ext = jnp.concatenate([top_ref[:, sl] * top_on, gate_ref[:, sl], bot_ref[:, sl] * bot_on], axis=0)
        left = jnp.where(col == 0, 0.0, pltpu.roll(ext, 1, 0))
        right = jnp.where(col == width - 1, 0.0, pltpu.roll(ext, n_ext - 1, 0))
        acc = jnp.broadcast_to(cb_ref[:, sl], (tm, chunk))
        for dy in range(3):
            rs = slice(dy * width, dy * width + tm)
            acc = acc + cw_ref[3 * dy:3 * dy + 1, sl] * left[rs]
            acc = acc + cw_ref[3 * dy + 1:3 * dy + 2, sl] * ext[rs]
            acc = acc + cw_ref[3 * dy + 2:3 * dy + 3, sl] * right[rs]
        act_ref[:, sl] = (_gelu_tanh(acc) * val_ref[:, sl]).astype(BF16)
    f = jnp.dot(act_ref[...], wd_ref[...], preferred_element_type=F32)
    o_ref[...] = x_ref[...] + mod_ref[5:6, :] * (_rms(f) * g_ref[...])


def _ffn_down(up, x, mods, mod_row, conv_w, conv_b, w_down, g, *, width):
    bsz, rows, n2 = up.shape
    f = n2 // 2
    d = x.shape[2]
    tm = ROW_TILE
    rpt = tm // width
    nrast = rows // width
    full = lambda shape: pl.BlockSpec(shape, lambda b, t: (0,) * len(shape))
    return pl.pallas_call(
        functools.partial(_ffn_down_kernel, width=width, chunk=256),
        grid=(bsz, rows // tm),
        in_specs=[pl.BlockSpec((None, tm, f), lambda b, t: (b, t, 0)),
                  pl.BlockSpec((None, tm, f), lambda b, t: (b, t, 1)),
                  pl.BlockSpec((None, width, f), lambda b, t: (b, jnp.maximum(t * rpt - 1, 0), 0)),
                  pl.BlockSpec((None, width, f), lambda b, t: (b, jnp.minimum((t + 1) * rpt, nrast - 1), 0)),
                  full((9, f)), full((1, f)), full((f, d)),
                  pl.BlockSpec((None, tm, d), lambda b, t: (b, t, 0)),
                  pl.BlockSpec((None, 6, d), lambda b, t: (mod_row(b, t), 0, 0)),
                  full((1, d))],
        out_specs=pl.BlockSpec((None, tm, d), lambda b, t: (b, t, 0)),
        out_shape=jax.ShapeDtypeStruct((bsz, rows, d), F32),
        scratch_shapes=[pltpu.VMEM((tm, f), BF16)],
        compiler_params=_params(("arbitrary", "arbitrary")),
        name="ffn_down",
    )(up, up, up, up, conv_w.reshape(9, f), conv_b.reshape(1, f), w_down, x, mods, g.reshape(1, d))


def _grid_transpose(t, rows, cols):
    b = t.shape[0]
    rest = t.shape[2:]
    return jnp.swapaxes(t.reshape((b, rows, cols) + rest), 1, 2).reshape((b, rows * cols) + rest)


def kernel(x, c, ctx, c_ctx, ada_w, ada_b, norm_mix_pre, norm_mix_post, norm_ffn_pre, norm_ffn_post, w_in, ssm_conv_w, ssm_conv_b, ssm_dt_bias, ssm_a_log, ssm_d, ssm_norm, ssm_out, rwkv_conv_w, rwkv_w0, rwkv_w2, rwkv_a0, rwkv_a2, rwkv_g2, rwkv_k_k, rwkv_k_a, rwkv_r_k, rwkv_ln_w, rwkv_ln_b, rwkv_out, w_o, ffn_w_in, ffn_conv_w, ffn_conv_b, ffn_w_out):
    bsz, seq, d = x.shape
    depth = ada_w.shape[0]
    rows = seq // GRID_W
    ssm_heads = ssm_d.shape[1]
    d_ssm = ssm_heads * SSM_HEAD_DIM
    d_xbc = d_ssm + 2 * SSM_GROUPS * SSM_STATE
    lora_g = rwkv_g2.shape[1]
    ctx_tiles = CTX_LEN // ROW_TILE
    ctx_row = bsz

    cols = {"z": 0, "xbc": d_ssm, "rkv": d_ssm + d_xbc}
    cols["gate"] = cols["rkv"] + 3 * d
    cols["w_dn"] = cols["gate"] + 2 * d
    cols["a_dn"] = cols["w_dn"] + 2 * LORA
    cols["g_dn"] = cols["a_dn"] + 2 * LORA
    cols["dt"] = cols["g_dn"] + lora_g
    o_z, o_xbc, o_dt, o_rkv = 0, d_ssm, d_ssm + d_xbc, d_ssm + d_xbc + ssm_heads
    o_wdn = o_rkv + 3 * d
    o_adn = o_wdn + 2 * LORA
    o_gdn = o_adn + 2 * LORA
    o_gate = o_gdn + lora_g

    cond = jnp.concatenate([c, c_ctx[None, :], jnp.zeros((16 - bsz - 1, d), F32)], axis=0)
    seq_row = lambda b, t: jnp.where(t < ctx_tiles, ctx_row, b)
    lat_row = lambda b, t: b
    ctx_only_row = lambda b, t: ctx_row

    xl, xc = x, ctx
    for i in range(depth):
        last = i == depth - 1
        col_major = i % 2 == 1
        mods = _adaln(cond, ada_w[i], ada_b[i]).reshape(16, 6, d)

        wi = w_in[i]
        w_cat = jnp.concatenate(
            [wi[:, o_z:o_xbc], wi[:, o_xbc:o_dt], wi[:, o_rkv:o_wdn], wi[:, o_gate:o_gate + 2 * d],
             wi[:, o_wdn:o_adn], wi[:, o_adn:o_gdn], wi[:, o_gdn:o_gate], wi[:, o_dt:o_rkv],
             jnp.zeros((d, LANES - ssm_heads), F32)], axis=1).astype(BF16)

        xl_in = _grid_transpose(xl, rows, GRID_W) if col_major else xl
        x_cat = jnp.concatenate([xc, xl_in], axis=1)
        proj = _norm_proj(x_cat, mods, ctx_row, norm_mix_pre[i], w_cat, tn=w_cat.shape[1] // 7)

        pad_heads = lambda a: jnp.pad(a, ((0, 0), (0, LANES - ssm_heads))).reshape(2, 1, LANES)
        d_skip = jnp.repeat(ssm_d[i], SSM_HEAD_DIM).reshape(1, d_ssm)
        y_ssd_f, y_ssd_b = _ssd_scan(proj, cols, ssm_conv_w[i], ssm_conv_b[i], pad_heads(ssm_dt_bias[i]),
                                     pad_heads(ssm_a_log[i]), d_skip)

        zpad = jnp.zeros((LORA, d), F32)
        w2p = jnp.stack([jnp.concatenate([rwkv_w2[i, 0], zpad], 0), jnp.concatenate([zpad, rwkv_w2[i, 1]], 0)])
        a2p = jnp.stack([jnp.concatenate([rwkv_a2[i, 0], zpad], 0), jnp.concatenate([zpad, rwkv_a2[i, 1]], 0)])
        y_rw, bonus = _rwkv_scan(proj, cols, rwkv_conv_w[i], w2p.astype(BF16), a2p.astype(BF16),
                                 rwkv_w0[i].reshape(2, 1, d), rwkv_a0[i].reshape(2, 1, d),
                                 rwkv_k_k[i], rwkv_k_a[i], rwkv_r_k[i])

        mix = _merge(proj, y_ssd_f, y_ssd_b, y_rw, bonus, mods, seq_row, cols, ssm_norm[i],
                     ssm_out[i].astype(BF16), rwkv_ln_w[i], rwkv_ln_b[i], rwkv_g2[i].astype(BF16),
                     rwkv_out[i].astype(BF16), w_o[i].astype(BF16), norm_mix_post[i])

        w_up = ffn_w_in[i].astype(BF16)
        w_dn = ffn_w_out[i].astype(BF16)
        if col_major:
            xl, up = _ffn_up(xl, _grid_transpose(mix[:, CTX_LEN:], GRID_W, rows), 0, mods, lat_row,
                             norm_ffn_pre[i], w_up)
        else:
            xl, up = _ffn_up(xl, mix, CTX_LEN, mods, lat_row, norm_ffn_pre[i], w_up)
        xl = _ffn_down(up, xl, mods, lat_row, ffn_conv_w[i], ffn_conv_b[i], w_dn, norm_ffn_post[i], width=GRID_W)

        if not last:
            xc, up_c = _ffn_up(xc, mix, 0, mods, ctx_only_row, norm_ffn_pre[i], w_up)
            xc = _ffn_down(up_c, xc, mods, ctx_only_row, ffn_conv_w[i], ffn_conv_b[i], w_dn, norm_ffn_post[i],
                           width=CTX_LEN)
    return xl
```

```python
import functools

import jax
import jax.numpy as jnp
from jax import lax
from jax.experimental import pallas as pl
from jax.experimental.pallas import tpu as pltpu

F32 = jnp.float32
BF16 = jnp.bfloat16

GRID_W = 64
CTX_LEN = 256
NORM_EPS = 1e-6
RWKV_LN_EPS = 64e-5
LOG2_E = 1.4426950408889634

LANES = 128
SSM_HEAD_DIM = 64
SSM_GROUPS = 4
SSM_STATE = 128
SSM_CHUNK = 128
RWKV_HEAD = 64
RWKV_CHUNK = 64
RWKV_BATCH = 4
RWKV_WAVE = 2
LORA = 64

ROW_TILE = 256
VMEM_LIMIT = 56 * 1024 * 1024


def _params(sem):
    return pltpu.CompilerParams(dimension_semantics=sem, vmem_limit_bytes=VMEM_LIMIT)


def _bdot(a, b):
    return jnp.dot(a.astype(BF16), b.astype(BF16), preferred_element_type=F32)


def _bdot_nt(a, b):
    return lax.dot_general(a.astype(BF16), b.astype(BF16), (((1,), (1,)), ((), ())),
                           preferred_element_type=F32)


def _split(x, parts):
    out = []
    r = x
    for _ in range(parts):
        p = r.astype(BF16)
        out.append(p)
        r = r - p.astype(F32)
    return out


def _sel_dot(sel, x, parts):
    acc = None
    for p in _split(x, parts):
        t = jnp.dot(sel, p, preferred_element_type=F32)
        acc = t if acc is None else acc + t
    return acc


def _dot_sel(x, sel, parts):
    acc = None
    for p in _split(x, parts):
        t = jnp.dot(p, sel, preferred_element_type=F32)
        acc = t if acc is None else acc + t
    return acc


def _sigmoid(x):
    return 1.0 / (1.0 + jnp.exp2(x * -LOG2_E))


def _softplus(x):
    return jnp.maximum(x, 0.0) + jnp.log(1.0 + jnp.exp(-jnp.abs(x)))


def _rms(x):
    return x * lax.rsqrt(jnp.mean(x * x, axis=-1, keepdims=True) + NORM_EPS)


def _head_ones(width):
    r = lax.broadcasted_iota(jnp.int32, (width, width), 0) // RWKV_HEAD
    c = lax.broadcasted_iota(jnp.int32, (width, width), 1) // RWKV_HEAD
    return (r == c).astype(BF16)


def _adaln_kernel(c_ref, w_ref, b_ref, o_ref):
    c = c_ref[...]
    s = c * _sigmoid(c)
    acc = None
    w = w_ref[...]
    w_parts = _split(w, 3)
    s_parts = _split(s, 3)
    for i in range(3):
        for j in range(3 - i):
            t = jnp.dot(s_parts[i], w_parts[j], preferred_element_type=F32)
            acc = t if acc is None else acc + t
    o_ref[...] = acc + b_ref[...]


def _adaln(cond, w, b):
    rows, d = cond.shape
    n = w.shape[1]
    tn = n // 12
    return pl.pallas_call(
        _adaln_kernel,
        grid=(n // tn,),
        in_specs=[pl.BlockSpec((rows, d), lambda j: (0, 0)),
                  pl.BlockSpec((d, tn), lambda j: (0, j)),
                  pl.BlockSpec((1, tn), lambda j: (0, j))],
        out_specs=pl.BlockSpec((rows, tn), lambda j: (0, j)),
        out_shape=jax.ShapeDtypeStruct((rows, n), F32),
        compiler_params=_params(("arbitrary",)),
        name="adaln",
    )(cond, w, b.reshape(1, n))


def _norm_proj_kernel(x_ref, modb_ref, modc_ref, g_ref, w_ref, o_ref, *, ctx_rows):
    x = x_ref[...]
    tm = x.shape[0]
    h = _rms(x) * g_ref[...]
    is_ctx = pl.program_id(2) * tm + lax.broadcasted_iota(jnp.int32, (tm, 1), 0) < ctx_rows
    shift = jnp.where(is_ctx, modc_ref[0:1, :], modb_ref[0:1, :])
    scale = jnp.where(is_ctx, modc_ref[1:2, :], modb_ref[1:2, :])
    h = h * (1.0 + scale) + shift
    o_ref[...] = jnp.dot(h.astype(BF16), w_ref[...], preferred_element_type=F32)


def _row_tile(rows, cap):
    return max(t for t in range(8, cap + 1, 8) if rows % t == 0)


def _norm_proj(x, mods, ctx_row, g, w, *, tn):
    bsz, rows, d = x.shape
    n = w.shape[1]
    tm = _row_tile(rows, 1088)
    return pl.pallas_call(
        functools.partial(_norm_proj_kernel, ctx_rows=CTX_LEN),
        grid=(n // tn, bsz, rows // tm),
        in_specs=[pl.BlockSpec((None, tm, d), lambda j, b, t: (b, t, 0)),
                  pl.BlockSpec((None, 6, d), lambda j, b, t: (b, 0, 0)),
                  pl.BlockSpec((None, 6, d), lambda j, b, t: (ctx_row, 0, 0)),
                  pl.BlockSpec((1, d), lambda j, b, t: (0, 0)),
                  pl.BlockSpec((d, tn), lambda j, b, t: (0, j))],
        out_specs=pl.BlockSpec((None, tm, tn), lambda j, b, t: (b, t, j)),
        out_shape=jax.ShapeDtypeStruct((bsz, rows, n), F32),
        compiler_params=_params(("arbitrary", "arbitrary", "arbitrary")),
        name="norm_proj",
    )(x, mods, mods, g.reshape(1, d), w)


def _scan_chunk(d, s, ncx, nc):
    rev = jnp.where(s < ncx, ncx - 1 - s, nc - 1 + ncx - s)
    return jnp.where(d == 0, s, rev)


def _conv3(x, prev_row, next_row, w, wcol):
    rows = x.shape[0]
    sub = lax.broadcasted_iota(jnp.int32, (8, x.shape[1]), 0)
    xp = pltpu.roll(x, 1, 0)
    xn = pltpu.roll(x, rows - 1, 0)
    xp = jnp.concatenate([jnp.where(sub == 0, prev_row, xp[0:8]), xp[8:rows]], axis=0)
    xn = jnp.concatenate([xn[0:rows - 8], jnp.where(sub == 7, next_row, xn[rows - 8:rows])], axis=0)
    return w[0:1, wcol] * xp + w[1:2, wcol] * x + w[2:3, wcol] * xn


def _pair_expand(v, h0):
    rows = v.shape[0]
    lane = lax.broadcasted_iota(jnp.int32, (rows, LANES), 1)
    lo = jnp.broadcast_to(v[:, h0:h0 + 1], (rows, LANES))
    hi = jnp.broadcast_to(v[:, h0 + 1:h0 + 2], (rows, LANES))
    return jnp.where(lane < SSM_HEAD_DIM, lo, hi)


def _ssd_chunk(fwd, first_step, load_bc, load_xs, dt_ref, bias_ref, alog_ref, dskip_ref, o_ref, h_ref):
    q = SSM_CHUNK
    bc_w = SSM_GROUPS * SSM_STATE

    @pl.when(first_step)
    def _():
        h_ref[...] = jnp.zeros_like(h_ref)

    li = lax.broadcasted_iota(jnp.int32, (q, q), 0)
    si = lax.broadcasted_iota(jnp.int32, (q, q), 1)
    incl = si <= li if fwd else si >= li
    tri = jnp.where(incl, 1.0, 0.0).astype(BF16)
    half = lax.broadcasted_iota(jnp.int32, (q, LANES), 1) < SSM_HEAD_DIM

    dt = _softplus(dt_ref[...] + bias_ref[...])
    da = dt * (-LOG2_E * jnp.exp(alog_ref[...]))
    cs = _sel_dot(tri, da, 2)
    cs_end = cs[q - 1:q, :] if fwd else cs[0:1, :]
    c_dec = jnp.exp2(cs_end)
    cs_t = (cs - jnp.log2(dt)).T
    wend_t = (dt * jnp.exp2(cs_end - cs)).T

    for g in range(SSM_GROUPS):
        bg = load_bc(slice(g * SSM_STATE, (g + 1) * SSM_STATE))
        cg = load_bc(slice(bc_w + g * SSM_STATE, bc_w + (g + 1) * SSM_STATE))
        cb = _bdot_nt(cg, bg)
        bg_t = bg.T
        for pp in range(4):
            p = g * 4 + pp
            h0 = 2 * p
            sl = slice(p * LANES, (p + 1) * LANES)
            xs = load_xs(sl)
            h_old = h_ref[:, sl]
            lhs_y, lhs_h = [], []
            for hd in (h0, h0 + 1):
                dl = jnp.broadcast_to(cs[:, hd:hd + 1], (q, q))
                ds_ = jnp.broadcast_to(cs_t[hd:hd + 1, :], (q, q))
                lm = jnp.exp2(jnp.where(incl, dl - ds_, -1e30))
                lhs_y.append(jnp.concatenate([cb * lm, cg * jnp.exp2(dl)], axis=1).astype(BF16))
                lhs_h.append((bg_t * jnp.broadcast_to(wend_t[hd:hd + 1, :], (q, q))).astype(BF16))
            yy = jnp.dot(jnp.concatenate(lhs_y, axis=0), jnp.concatenate([xs, h_old], axis=0).astype(BF16),
                         preferred_element_type=F32)
            y = jnp.where(half, yy[0:q], yy[q:2 * q])
            o_ref[:, sl] = y if dskip_ref is None else y + dskip_ref[:, sl] * xs
            uu = jnp.dot(jnp.concatenate(lhs_h, axis=0), xs.astype(BF16), preferred_element_type=F32)
            h_ref[:, sl] = h_old * _pair_expand(c_dec, h0) + jnp.where(half, uu[0:q], uu[q:2 * q])


def _ssd_fwd_kernel(xs_ref, bc_ref, xsp_ref, xsn_ref, bcp_ref, bcn_ref, dt_ref, cw_ref, cbias_ref, bias_ref,
                    alog_ref, dskip_ref, o_ref, act_ref, h_ref, *, ncx):
    c = pl.program_id(1)
    nc = pl.num_programs(1)
    d_ssm = xs_ref.shape[1]
    seg_first = jnp.logical_or(c == 0, c == ncx)
    seg_last = jnp.logical_or(c == ncx - 1, c == nc - 1)

    def conv_act(x_ref, p_ref, n_ref, col, off):
        wcol = slice(off + col.start, off + col.stop)
        prev_row = jnp.where(seg_first, 0.0, p_ref[7:8, col])
        next_row = jnp.where(seg_last, 0.0, n_ref[0:1, col])
        y = _conv3(x_ref[:, col], prev_row, next_row, cw_ref, wcol) + cbias_ref[:, wcol]
        y = y * _sigmoid(y)
        act_ref[:, wcol] = y.astype(BF16)
        return y

    _ssd_chunk(True, c == 0, lambda col: conv_act(bc_ref, bcp_ref, bcn_ref, col, d_ssm),
               lambda col: conv_act(xs_ref, xsp_ref, xsn_ref, col, 0),
               dt_ref, bias_ref, alog_ref, dskip_ref, o_ref, h_ref)


def _ssd_bwd_kernel(xa_ref, bca_ref, dt_ref, bias_ref, alog_ref, o_ref, h_ref):
    _ssd_chunk(False, pl.program_id(1) == 0, lambda col: bca_ref[:, col].astype(F32),
               lambda col: xa_ref[:, col].astype(F32), dt_ref, bias_ref, alog_ref, None, o_ref, h_ref)


def _ssd_scan(proj, cols, conv_w, conv_b, dt_bias, a_log, d_skip):
    bsz, tt, _ = proj.shape
    q = SSM_CHUNK
    nc = tt // q
    ncx = CTX_LEN // q
    bc_w = SSM_GROUPS * SSM_STATE
    d_xbc = conv_w.shape[1]
    d_ssm = d_xbc - 2 * bc_w
    xs_cb = cols["xbc"] // d_ssm
    bc_cb = (cols["xbc"] + d_ssm) // (2 * bc_w)
    dt_cb = cols["dt"] // LANES
    hb = q // 8
    nhb = tt // 8
    full = lambda shape: pl.BlockSpec(shape, lambda b, s: (0,) * len(shape))
    head_row = lambda di: pl.BlockSpec((None, 1, LANES), lambda b, s: (di, 0, 0))
    prev = lambda s: jnp.maximum(s * hb - 1, 0)
    nxt = lambda s: jnp.minimum((s + 1) * hb, nhb - 1)
    y_fwd, act = pl.pallas_call(
        functools.partial(_ssd_fwd_kernel, ncx=ncx),
        grid=(bsz, nc),
        in_specs=[pl.BlockSpec((None, q, d_ssm), lambda b, s: (b, s, xs_cb)),
                  pl.BlockSpec((None, q, 2 * bc_w), lambda b, s: (b, s, bc_cb)),
                  pl.BlockSpec((None, 8, d_ssm), lambda b, s: (b, prev(s), xs_cb)),
                  pl.BlockSpec((None, 8, d_ssm), lambda b, s: (b, nxt(s), xs_cb)),
                  pl.BlockSpec((None, 8, 2 * bc_w), lambda b, s: (b, prev(s), bc_cb)),
                  pl.BlockSpec((None, 8, 2 * bc_w), lambda b, s: (b, nxt(s), bc_cb)),
                  pl.BlockSpec((None, q, LANES), lambda b, s: (b, s, dt_cb)),
                  full(conv_w.shape), full((1, d_xbc)), head_row(0), head_row(0), full((1, d_ssm))],
        out_specs=[pl.BlockSpec((None, q, d_ssm), lambda b, s: (b, s, 0)),
                   pl.BlockSpec((None, q, d_xbc), lambda b, s: (b, s, 0))],
        out_shape=[jax.ShapeDtypeStruct((bsz, tt, d_ssm), F32), jax.ShapeDtypeStruct((bsz, tt, d_xbc), BF16)],
        scratch_shapes=[pltpu.VMEM((SSM_STATE, d_ssm), F32)],
        compiler_params=_params(("arbitrary", "arbitrary")),
        name="ssd_fwd",
    )(proj, proj, proj, proj, proj, proj, proj, conv_w, conv_b.reshape(1, d_xbc), dt_bias, a_log, d_skip)

    ck = lambda s: _scan_chunk(1, s, ncx, nc)
    y_bwd = pl.pallas_call(
        _ssd_bwd_kernel,
        grid=(bsz, nc),
        in_specs=[pl.BlockSpec((None, q, d_ssm), lambda b, s: (b, ck(s), 0)),
                  pl.BlockSpec((None, q, 2 * bc_w), lambda b, s: (b, ck(s), d_ssm // (2 * bc_w))),
                  pl.BlockSpec((None, q, LANES), lambda b, s: (b, ck(s), dt_cb)),
                  head_row(1), head_row(1)],
        out_specs=pl.BlockSpec((None, q, d_ssm), lambda b, s: (b, ck(s), 0)),
        out_shape=jax.ShapeDtypeStruct((bsz, tt, d_ssm), F32),
        scratch_shapes=[pltpu.VMEM((SSM_STATE, d_ssm), F32)],
        compiler_params=_params(("arbitrary", "arbitrary")),
        name="ssd_bwd",
    )(act, act, proj, dt_bias, a_log)
    return y_fwd, y_bwd


def _rwkv_kernel(r_ref, k_ref, v_ref, rp_ref, kp_ref, vp_ref, rn_ref, kn_ref, vn_ref, wdn_ref, adn_ref,
                 cw_ref, w2_ref, a2_ref, w0_ref, a0_ref, kk_ref, ka_ref, rk_ref, y_ref, bonus_ref,
                 st_ref, nl_ref, nb_ref, *, ncx):
    d = pl.program_id(1)
    s = pl.program_id(2)
    nc = pl.num_programs(2)
    fwd = d == 0
    c_len = RWKV_CHUNK
    n2 = 2 * c_len
    dr = r_ref.shape[2]

    @pl.when(s == 0)
    def _():
        st_ref[...] = jnp.zeros_like(st_ref)
        nl_ref[...] = jnp.zeros_like(nl_ref)
        nb_ref[...] = jnp.zeros_like(nb_ref)

    ck = _scan_chunk(d, s, ncx, nc)
    seg_first = jnp.logical_or(ck == 0, ck == ncx)
    seg_last = jnp.logical_or(ck == ncx - 1, ck == nc - 1)

    def conv(x_ref, p_ref, n_ref, bi, sl, which):
        prev_row = jnp.where(seg_first, 0.0, p_ref[bi, 7:8, sl])
        next_row = jnp.where(seg_last, 0.0, n_ref[bi, 0:1, sl])
        return _conv3(x_ref[bi, :, sl], prev_row, next_row, cw_ref,
                      slice(which * dr + sl.start, which * dr + sl.stop))

    qw = 2 * LANES
    ti = lax.broadcasted_iota(jnp.int32, (c_len, c_len), 0)
    ui = lax.broadcasted_iota(jnp.int32, (c_len, c_len), 1)
    sgn = jnp.where(fwd, 1, -1)
    tri = jnp.where(sgn * (ui - ti) <= 0, 1.0, 0.0).astype(BF16)
    ones4 = _head_ones(qw)
    even_head = (lax.broadcasted_iota(jnp.int32, (c_len, qw), 1) // RWKV_HEAD) % 2 == 0

    rr = lax.broadcasted_iota(jnp.int32, (n2, n2), 0)
    cc = lax.broadcasted_iota(jnp.int32, (n2, n2), 1)
    same = (rr // c_len) == (cc // c_len)
    order = sgn * (cc % c_len - rr % c_len)
    strict = jnp.logical_and(same, order < 0)
    incl = jnp.logical_and(same, order <= 0)
    eye_q = lax.broadcasted_iota(jnp.int32, (n2, qw), 0) == lax.broadcasted_iota(jnp.int32, (n2, qw), 1) % n2
    eye_f = jnp.where(eye_q, 1.0, 0.0)
    pair_lanes = (slice(0, LANES), slice(LANES, qw))

    nb = r_ref.shape[0]
    n_quads = r_ref.shape[2] // qw
    tanh_w = [jnp.tanh(wdn_ref[bi]) for bi in range(nb)]
    adn = [adn_ref[bi] for bi in range(nb)]

    def stack(x):
        return jnp.concatenate([jnp.where(even_head, x, 0.0), jnp.where(even_head, 0.0, x)], axis=0)

    def bdiag(m):
        z = jnp.zeros((m.shape[0], LANES), m.dtype)
        return jnp.concatenate([jnp.concatenate([m[:, 0:LANES], z], axis=1),
                                jnp.concatenate([z, m[:, LANES:qw]], axis=1)], axis=0)

    def lanes2(parts):
        return jnp.concatenate(parts, axis=1)

    def pair_transposed(m):
        return lanes2([m[:, lj].T for lj in pair_lanes])

    def prep(q, e):
        bi, qd = divmod(q, n_quads)
        sl = slice(qd * qw, (qd + 1) * qw)
        r = conv(r_ref, rp_ref, rn_ref, bi, sl, 0)
        k = conv(k_ref, kp_ref, kn_ref, bi, sl, 1)
        v = conv(v_ref, vp_ref, vn_ref, bi, sl, 2)
        z = w0_ref[:, sl] + _bdot(tanh_w[bi], w2_ref[:, sl])
        lw = -jnp.exp(-_softplus(-z) - 0.5)
        a_sig = _sigmoid(a0_ref[:, sl] + _bdot(adn[bi], a2_ref[:, sl]))
        kd = k * (1.0 + (a_sig - 1.0) * ka_ref[:, sl])
        kkv = k * kk_ref[:, sl]
        nrm = jnp.sqrt(_dot_sel(kkv * kkv, ones4, 1))
        kk = kkv / jnp.maximum(nrm, 1e-12)
        bvec = kk * a_sig
        bonus_ref[bi, :, sl] = _dot_sel(r * kd * rk_ref[:, sl], ones4, 1) * v

        c = _sel_dot(tri, lw, 2)
        e_nc = jnp.exp(-c)
        pe = jnp.exp(jnp.where(fwd, c[c_len - 1:c_len, :], c[0:1, :]))
        k_t = kd * e_nc
        b_t = bvec * e_nc
        e["at"] = stack(-kk * jnp.exp(c - lw)).astype(BF16)
        e["rt"] = stack(r * jnp.exp(c))
        e["bk"] = jnp.concatenate([stack(b_t), stack(k_t)], axis=0).astype(BF16)
        e["v"] = stack(v).astype(BF16)
        e["kc"] = pair_transposed(stack(k_t * pe)).astype(BF16)
        e["bc"] = pair_transposed(stack(b_t * pe)).astype(BF16)
        e["pe"] = pe

    def products(q, e):
        rt_b = e["rt"].astype(BF16)
        a_ab, a_k, a_rb = [], [], []
        for lj in pair_lanes:
            lhs = jnp.concatenate([e["at"][:, lj], rt_b[:, lj]], axis=0)
            big = lax.dot_general(lhs, e["bk"][:, lj], (((1,), (1,)), ((), ())), preferred_element_type=F32)
            a_ab.append(jnp.where(strict, big[0:n2, 0:n2], 0.0))
            a_k.append(jnp.concatenate([jnp.where(strict, big[0:n2, n2:2 * n2], 0.0),
                                        jnp.where(incl, big[n2:2 * n2, n2:2 * n2], 0.0)], axis=0).astype(BF16))
            a_rb.append(jnp.where(incl, big[n2:2 * n2, 0:n2], 0.0).astype(BF16))
        a_ab = lanes2(a_ab)
        e["a_k"] = lanes2(a_k)
        e["a_rb"] = a_rb
        e["inv"] = eye_f + a_ab
        put_power(q, e["inv"], a_ab.astype(BF16))

    def put_power(q, inv, pw):
        if inv is not None:
            nl_ref[q, 0:n2, :] = inv.astype(BF16)
        nl_ref[q, n2:2 * n2, :] = pw
        nb_ref[q, 0:n2, 0:LANES] = pw[:, 0:LANES]
        nb_ref[q, n2:2 * n2, LANES:qw] = pw[:, LANES:qw]

    def square(q, e):
        pw = jnp.dot(nl_ref[q, n2:2 * n2, :], nb_ref[q], preferred_element_type=F32).astype(BF16)
        put_power(q, None, pw)
        kv =jnp.dot(jnp.concatenate([e["a_k"], e["kc"]], axis=0), bdiag(e["v"]),
                     preferred_element_type=F32)
        e["akv"] = kv[0:n2].astype(BF16)
        e["rkv"] = kv[n2:2 * n2]
        e["kc_v"] = kv[2 * n2:3 * n2]

    def level(q, e):
        both = jnp.dot(nl_ref[q], nb_ref[q], preferred_element_type=F32)
        e["inv"] = e["inv"] + both[0:n2]
        put_power(q, e["inv"], both[n2:2 * n2].astype(BF16))

    def last_level(q, e):
        e["inv"] = (e["inv"] + jnp.dot(nl_ref[q, 0:n2, :], nb_ref[q],
                                       preferred_element_type=F32)).astype(BF16)

    def solve(q, e):
        e["wu"] = [jnp.dot(e["inv"][:, lj], lanes2([e["at"][:, lj], e["akv"][:, lj]]),
                           preferred_element_type=F32).astype(BF16) for lj in pair_lanes]

    def apply(q, e):
        both = [jnp.dot(jnp.concatenate([e["a_rb"][j], e["bc"][:, lj]], axis=0), e["wu"][j],
                        preferred_element_type=F32) for j, lj in enumerate(pair_lanes)]
        e["rb_wu"] = [t[0:n2] for t in both]
        e["bc_wu"] = [t[n2:2 * n2] for t in both]

    def state(q, e):
        bi, qd = divmod(q, n_quads)
        sl = slice(qd * qw, (qd + 1) * qw)
        r_w = e["rt"] + lanes2([t[:, 0:n2] for t in e["rb_wu"]])
        g_m = jnp.where(eye_q, jnp.broadcast_to(e["pe"], (n2, qw)), 0.0) + lanes2([t[:, 0:n2] for t in e["bc_wu"]])
        y0 = e["rkv"] + lanes2([t[:, n2:2 * n2] for t in e["rb_wu"]])
        h_m = e["kc_v"] + lanes2([t[:, n2:2 * n2] for t in e["bc_wu"]])
        st = st_ref[q]
        both = jnp.dot(jnp.concatenate([r_w, g_m], axis=0).astype(BF16), bdiag(st.astype(BF16)),
                       preferred_element_type=F32)
        ys = both[0:n2] + y0
        st_ref[q] = both[n2:2 * n2] + h_m
        y_ref[bi, :, sl] = ys[0:c_len, :] + ys[c_len:n2, :]

    stages = [prep, products, square, level, level, level, level, last_level, solve, apply, state]
    n_items = nb * n_quads
    env = [dict() for _ in range(n_items)]
    for step in range((n_items - 1) // RWKV_WAVE + len(stages)):
        for q in range(n_items):
            if 0 <= step - q // RWKV_WAVE < len(stages):
                stages[step - q // RWKV_WAVE](q, env[q])


def _rwkv_scan(proj, cols, conv_w, w2p, a2p, w0, a0, k_k, k_a, r_k):
    bsz, tt, _ = proj.shape
    dr = conv_w.shape[1] // 3
    cl = RWKV_CHUNK
    nc = tt // cl
    ncx = CTX_LEN // cl
    cb0 = cols["rkv"] // dr
    hb = cl // 8
    nhb = tt // 8
    ck = lambda d, s: _scan_chunk(d, s, ncx, nc)
    prev = lambda d, s: jnp.maximum(ck(d, s) * hb - 1, 0)
    nxt = lambda d, s: jnp.minimum((ck(d, s) + 1) * hb, nhb - 1)

    nb = RWKV_BATCH if bsz % RWKV_BATCH == 0 else 1
    n_items = nb * dr // (2 * LANES)
    row = lambda: pl.BlockSpec((1, dr), lambda b, d, s: (0, 0))
    main = lambda j: pl.BlockSpec((nb, cl, dr), lambda b, d, s: (b, ck(d, s), cb0 + j))
    halo_p = lambda j: pl.BlockSpec((nb, 8, dr), lambda b, d, s: (b, prev(d, s), cb0 + j))
    halo_n = lambda j: pl.BlockSpec((nb, 8, dr), lambda b, d, s: (b, nxt(d, s), cb0 + j))
    out_spec = pl.BlockSpec((None, nb, cl, dr), lambda b, d, s: (d, b, ck(d, s), 0))
    return pl.pallas_call(
        functools.partial(_rwkv_kernel, ncx=ncx),
        grid=(bsz // nb, 2, nc),
        in_specs=[main(0), main(1), main(2), halo_p(0), halo_p(1), halo_p(2), halo_n(0), halo_n(1), halo_n(2),
                  pl.BlockSpec((nb, cl, LANES), lambda b, d, s: (b, ck(d, s), cols["w_dn"] // LANES)),
                  pl.BlockSpec((nb, cl, LANES), lambda b, d, s: (b, ck(d, s), cols["a_dn"] // LANES)),
                  pl.BlockSpec(conv_w.shape, lambda b, d, s: (0, 0)),
                  pl.BlockSpec((None, LANES, dr), lambda b, d, s: (d, 0, 0)),
                  pl.BlockSpec((None, LANES, dr), lambda b, d, s: (d, 0, 0)),
                  pl.BlockSpec((None, 1, dr), lambda b, d, s: (d, 0, 0)),
                  pl.BlockSpec((None, 1, dr), lambda b, d, s: (d, 0, 0)),
                  row(), row(), row()],
        out_specs=[out_spec, out_spec],
        out_shape=[jax.ShapeDtypeStruct((2, bsz, tt, dr), F32)] * 2,
        scratch_shapes=[pltpu.VMEM((n_items, LANES, 2 * LANES), F32),
                        pltpu.VMEM((n_items, 2 * LANES, 2 * LANES), BF16),
                        pltpu.VMEM((n_items, 2 * LANES, 2 * LANES), BF16)],
        compiler_params=_params(("arbitrary", "arbitrary", "arbitrary")),
        name="rwkv_scan",
    )(*([proj] * 11), conv_w, w2p, a2p, w0, a0, k_k.reshape(1, dr), k_a.reshape(1, dr), r_k.reshape(1, dr))


def _merge_kernel(z_ref, yf_ref, yb_ref, r0_ref, r1_ref, b0_ref, b1_ref, gdn_ref, gate_ref,
                  mod_ref, snorm_ref, sout_ref, lnw_ref, lnb_ref, g2_ref, rout_ref, wo_ref, gpost_ref, o_ref):
    d_model = o_ref.shape[1]
    z = z_ref[...]
    y = yf_ref[...] + yb_ref[...]
    y = y * (z * _sigmoid(z))
    gw = y.shape[1] // SSM_GROUPS
    yn = jnp.concatenate([_rms(y[:, g * gw:(g + 1) * gw]) for g in range(SSM_GROUPS)], axis=1)
    out_ssm = _bdot(yn * snorm_ref[...], sout_ref[...])

    yh = r0_ref[...] + r1_ref[...]
    ones4 = _head_ones(2 * LANES)
    pieces = []
    for p in range(d_model // (2 * LANES)):
        t = yh[:, p * 2 * LANES:(p + 1) * 2 * LANES]
        mu = _dot_sel(t, ones4, 1) * (1.0 / RWKV_HEAD)
        tc = t - mu
        var = _dot_sel(tc * tc, ones4, 1) * (1.0 / RWKV_HEAD)
        pieces.append(tc * lax.rsqrt(var + RWKV_LN_EPS))
    y_rw = jnp.concatenate(pieces, axis=1) * lnw_ref[...] + lnb_ref[...] + b0_ref[...] + b1_ref[...]
    g = _bdot(_sigmoid(gdn_ref[...]), g2_ref[...])
    out_rw = _bdot(y_rw * g, rout_ref[...])

    gate = _sigmoid(gate_ref[...])
    merged = gate[:, :d_model] * out_ssm + gate[:, d_model:] * out_rw
    yo = _bdot(merged, wo_ref[...])
    o_ref[...] = mod_ref[2:3, :] * (_rms(yo) * gpost_ref[...])


def _merge(proj, y_ssd_f, y_ssd_b, y_rw, bonus, mods, mod_row, cols, ssm_norm, ssm_out, ln_w, ln_b, g2, rwkv_out,
           w_o, g_post):
    bsz, tt, _ = proj.shape
    d_ssm = y_ssd_f.shape[-1]
    d = y_rw.shape[-1]
    tm = ROW_TILE
    full = lambda shape: pl.BlockSpec(shape, lambda b, t: (0,) * len(shape))
    dir_spec = lambda di, w: pl.BlockSpec((None, None, tm, w), lambda b, t: (di, b, t, 0))
    row_spec = lambda w: pl.BlockSpec((None, tm, w), lambda b, t: (b, t, 0))
    return pl.pallas_call(
        _merge_kernel,
        grid=(bsz, tt // tm),
        in_specs=[pl.BlockSpec((None, tm, d_ssm), lambda b, t: (b, t, cols["z"] // d_ssm)),
                  row_spec(d_ssm), row_spec(d_ssm),
                  dir_spec(0, d), dir_spec(1, d), dir_spec(0, d), dir_spec(1, d),
                  pl.BlockSpec((None, tm, LANES), lambda b, t: (b, t, cols["g_dn"] // LANES)),
                  pl.BlockSpec((None, tm, 2 * d), lambda b, t: (b, t, cols["gate"] // (2 * d))),
                  pl.BlockSpec((None, 6, d), lambda b, t: (mod_row(b, t), 0, 0)),
                  full((1, d_ssm)), full((d_ssm, d)),
                  full((1, d)), full((1, d)), full((LANES, d)), full((d, d)), full((d, d)), full((1, d))],
        out_specs=pl.BlockSpec((None, tm, d), lambda b, t: (b, t, 0)),
        out_shape=jax.ShapeDtypeStruct((bsz, tt, d), F32),
        compiler_params=_params(("arbitrary", "arbitrary")),
        name="merge",
    )(proj, y_ssd_f, y_ssd_b, y_rw, y_rw, bonus, bonus, proj, proj, mods,
      ssm_norm.reshape(1, d_ssm), ssm_out, ln_w.reshape(1, d), ln_b.reshape(1, d), g2,
      rwkv_out, w_o, g_post.reshape(1, d))


def _ffn_up_kernel(x_ref, mix_ref, mod_ref, g_ref, w_ref, xo_ref, up_ref):
    xn = x_ref[...] + mix_ref[...]
    xo_ref[...] = xn
    h = _rms(xn) * g_ref[...]
    h = h * (1.0 + mod_ref[4:5, :]) + mod_ref[3:4, :]
    up_ref[...] = jnp.dot(h.astype(BF16), w_ref[...], preferred_element_type=F32)


def _ffn_up(x, mix, mix_row0, mods, mod_row, g, w):
    bsz, rows, d = x.shape
    n = w.shape[1]
    tm = ROW_TILE
    t0 = mix_row0 // tm
    return pl.pallas_call(
        _ffn_up_kernel,
        grid=(bsz, rows // tm),
        in_specs=[pl.BlockSpec((None, tm, d), lambda b, t: (b, t, 0)),
                  pl.BlockSpec((None, tm, d), lambda b, t: (b, t0 + t, 0)),
                  pl.BlockSpec((None, 6, d), lambda b, t: (mod_row(b, t), 0, 0)),
                  pl.BlockSpec((1, d), lambda b, t: (0, 0)),
                  pl.BlockSpec((d, n), lambda b, t: (0, 0))],
        out_specs=[pl.BlockSpec((None, tm, d), lambda b, t: (b, t, 0)),
                   pl.BlockSpec((None, tm, n), lambda b, t: (b, t, 0))],
        out_shape=[jax.ShapeDtypeStruct((bsz, rows, d), F32), jax.ShapeDtypeStruct((bsz, rows, n), F32)],
        compiler_params=_params(("arbitrary", "arbitrary")),
        name="ffn_up",
    )(x, mix, mods, g.reshape(1, d), w)


def _gelu_tanh(x):
    h = 0.5 * x
    return h + h * jnp.tanh(x * (0.7978845608028654 + 0.035677408136300125 * (x * x)))


def _ffn_down_kernel(gate_ref, val_ref, top_ref, bot_ref, cw_ref, cb_ref, wd_ref, x_ref, mod_ref, g_ref,
                     o_ref, act_ref, *, width, chunk):
    t = pl.program_id(1)
    nt = pl.num_programs(1)
    tm, ch = gate_ref.shape
    n_ext = tm + 2 * width
    col = lax.broadcasted_iota(jnp.int32, (n_ext, chunk), 0) % width
    top_on = jnp.where(t == 0, 0.0, 1.0)
    bot_on = jnp.where(t == nt - 1, 0.0, 1.0)
    for c in range(ch // chunk):
        sl = slice(c * chunk, (c + 1) * chunk)
        ext = jnp.concatenate([top_ref[:, sl] * top_on, gate_ref[:, sl], bot_ref[:, sl] * bot_on], axis=0)
        left = jnp.where(col == 0, 0.0, pltpu.roll(ext, 1, 0))
        right = jnp.where(col == width - 1, 0.0, pltpu.roll(ext, n_ext - 1, 0))
        acc = jnp.broadcast_to(cb_ref[:, sl], (tm, chunk))
        for dy in range(3):
            rs = slice(dy * width, dy * width + tm)
            acc = acc + cw_ref[3 * dy:3 * dy + 1, sl] * left[rs]
            acc = acc + cw_ref[3 * dy + 1:3 * dy + 2, sl] * ext[rs]
            acc = acc + cw_ref[3 * dy + 2:3 * dy + 3, sl] * right[rs]
        act_ref[:, sl] = (_gelu_tanh(acc) * val_ref[:, sl]).astype(BF16)
    f = jnp.dot(act_ref[...], wd_ref[...], preferred_element_type=F32)
    o_ref[...] = x_ref[...] + mod_ref[5:6, :] * (_rms(f) * g_ref[...])


def _ffn_down(up, x, mods, mod_row, conv_w, conv_b, w_down, g, *, width):
    bsz, rows, n2 = up.shape
    f = n2 // 2
    d = x.shape[2]
    tm = ROW_TILE
    rpt = tm // width
    nrast = rows // width
    full = lambda shape: pl.BlockSpec(shape, lambda b, t: (0,) * len(shape))
    return pl.pallas_call(
        functools.partial(_ffn_down_kernel, width=width, chunk=256),
        grid=(bsz, rows // tm),
        in_specs=[pl.BlockSpec((None, tm, f), lambda b, t: (b, t, 0)),
                  pl.BlockSpec((None, tm, f), lambda b, t: (b, t, 1)),
                  pl.BlockSpec((None, width, f), lambda b, t: (b, jnp.maximum(t * rpt - 1, 0), 0)),
                  pl.BlockSpec((None, width, f), lambda b, t: (b, jnp.minimum((t + 1) * rpt, nrast - 1), 0)),
                  full((9, f)), full((1, f)), full((f, d)),
                  pl.BlockSpec((None, tm, d), lambda b, t: (b, t, 0)),
                  pl.BlockSpec((None, 6, d), lambda b, t: (mod_row(b, t), 0, 0)),
                  full((1, d))],
        out_specs=pl.BlockSpec((None, tm, d), lambda b, t: (b, t, 0)),
        out_shape=jax.ShapeDtypeStruct((bsz, rows, d), F32),
        scratch_shapes=[pltpu.VMEM((tm, f), BF16)],
        compiler_params=_params(("arbitrary", "arbitrary")),
        name="ffn_down",
    )(up, up, up, up, conv_w.reshape(9, f), conv_b.reshape(1, f), w_down, x, mods, g.reshape(1, d))


def _grid_transpose(t, rows, cols):
    b = t.shape[0]
    rest = t.shape[2:]
    return jnp.swapaxes(t.reshape((b, rows, cols) + rest), 1, 2).reshape((b, rows * cols) + rest)


def kernel(x, c, ctx, c_ctx, ada_w, ada_b, norm_mix_pre, norm_mix_post, norm_ffn_pre, norm_ffn_post, w_in, ssm_conv_w, ssm_conv_b, ssm_dt_bias, ssm_a_log, ssm_d, ssm_norm, ssm_out, rwkv_conv_w, rwkv_w0, rwkv_w2, rwkv_a0, rwkv_a2, rwkv_g2, rwkv_k_k, rwkv_k_a, rwkv_r_k, rwkv_ln_w, rwkv_ln_b, rwkv_out, w_o, ffn_w_in, ffn_conv_w, ffn_conv_b, ffn_w_out):
    bsz, seq, d = x.shape
    depth = ada_w.shape[0]
    rows = seq // GRID_W
    ssm_heads = ssm_d.shape[1]
    d_ssm = ssm_heads * SSM_HEAD_DIM
    d_xbc = d_ssm + 2 * SSM_GROUPS * SSM_STATE
    lora_g = rwkv_g2.shape[1]
    ctx_tiles = CTX_LEN // ROW_TILE
    ctx_row = bsz

    cols = {"z": 0, "xbc": d_ssm, "rkv": d_ssm + d_xbc}
    cols["gate"] = cols["rkv"] + 3 * d
    cols["w_dn"] = cols["gate"] + 2 * d
    cols["a_dn"] = cols["w_dn"] + 2 * LORA
    cols["g_dn"] = cols["a_dn"] + 2 * LORA
    cols["dt"] = cols["g_dn"] + lora_g
    o_z, o_xbc, o_dt, o_rkv = 0, d_ssm, d_ssm + d_xbc, d_ssm + d_xbc + ssm_heads
    o_wdn = o_rkv + 3 * d
    o_adn = o_wdn + 2 * LORA
    o_gdn = o_adn + 2 * LORA
    o_gate = o_gdn + lora_g

    cond = jnp.concatenate([c, c_ctx[None, :], jnp.zeros((16 - bsz - 1, d), F32)], axis=0)
    seq_row = lambda b, t: jnp.where(t < ctx_tiles, ctx_row, b)
    lat_row = lambda b, t: b
    ctx_only_row = lambda b, t: ctx_row

    xl, xc = x, ctx
    for i in range(depth):
        last = i == depth - 1
        col_major = i % 2 == 1
        mods = _adaln(cond, ada_w[i], ada_b[i]).reshape(16, 6, d)

        wi = w_in[i]
        w_cat = jnp.concatenate(
            [wi[:, o_z:o_xbc], wi[:, o_xbc:o_dt], wi[:, o_rkv:o_wdn], wi[:, o_gate:o_gate + 2 * d],
             wi[:, o_wdn:o_adn], wi[:, o_adn:o_gdn], wi[:, o_gdn:o_gate], wi[:, o_dt:o_rkv],
             jnp.zeros((d, LANES - ssm_heads), F32)], axis=1).astype(BF16)

        xl_in = _grid_transpose(xl, rows, GRID_W) if col_major else xl
        x_cat = jnp.concatenate([xc, xl_in], axis=1)
        proj = _norm_proj(x_cat, mods, ctx_row, norm_mix_pre[i], w_cat, tn=w_cat.shape[1] // 7)

        pad_heads = lambda a: jnp.pad(a, ((0, 0), (0, LANES - ssm_heads))).reshape(2, 1, LANES)
        d_skip = jnp.repeat(ssm_d[i], SSM_HEAD_DIM).reshape(1, d_ssm)
        y_ssd_f, y_ssd_b = _ssd_scan(proj, cols, ssm_conv_w[i], ssm_conv_b[i], pad_heads(ssm_dt_bias[i]),
                                     pad_heads(ssm_a_log[i]), d_skip)

        zpad = jnp.zeros((LORA, d), F32)
        w2p = jnp.stack([jnp.concatenate([rwkv_w2[i, 0], zpad], 0), jnp.concatenate([zpad, rwkv_w2[i, 1]], 0)])
        a2p = jnp.stack([jnp.concatenate([rwkv_a2[i, 0], zpad], 0), jnp.concatenate([zpad, rwkv_a2[i, 1]], 0)])
        y_rw, bonus = _rwkv_scan(proj, cols, rwkv_conv_w[i], w2p.astype(BF16), a2p.astype(BF16),
                                 rwkv_w0[i].reshape(2, 1, d), rwkv_a0[i].reshape(2, 1, d),
                                 rwkv_k_k[i], rwkv_k_a[i], rwkv_r_k[i])

        mix = _merge(proj, y_ssd_f, y_ssd_b, y_rw, bonus, mods, seq_row, cols, ssm_norm[i],
                     ssm_out[i].astype(BF16), rwkv_ln_w[i], rwkv_ln_b[i], rwkv_g2[i].astype(BF16),
                     rwkv_out[i].astype(BF16), w_o[i].astype(BF16), norm_mix_post[i])

        w_up = ffn_w_in[i].astype(BF16)
        w_dn = ffn_w_out[i].astype(BF16)
        if col_major:
            xl, up = _ffn_up(xl, _grid_transpose(mix[:, CTX_LEN:], GRID_W, rows), 0, mods, lat_row,
                             norm_ffn_pre[i], w_up)
        else:
            xl, up = _ffn_up(xl, mix, CTX_LEN, mods, lat_row, norm_ffn_pre[i], w_up)
        xl = _ffn_down(up, xl, mods, lat_row, ffn_conv_w[i], ffn_conv_b[i], w_dn, norm_ffn_post[i], width=GRID_W)

        if not last:
            xc, up_c = _ffn_up(xc, mix, 0, mods, ctx_only_row, norm_ffn_pre[i], w_up)
            xc = _ffn_down(up_c, xc, mods, ctx_only_row, ffn_conv_w[i], ffn_conv_b[i], w_dn, norm_ffn_post[i],
                           width=CTX_LEN)
    return xl
```
